```python
import jax, jax.numpy as jnp
from jax import lax
import numpy as np

D_MODEL = 1024
BATCH = 2
SEQ = 8192
DEPTH = 1

CTX_LEN = 256
GRID_W = 64
ATT_HEADS = 8
ATT_KV_HEADS = 2
ATT_HEAD_DIM = 64
GLA_HEADS = 4
GLA_KEY_DIM = 64
GLA_VAL_DIM = 128
GLA_GATE_RANK = 16
GLA_GATE_NORMALIZER = 16.0
GLA_CHUNK = 64
D_FF = 2816
CONV_WIDTH = 3
Q_BLOCK = 128
ROPE_THETA = 10000.0
EPS = 1e-6

ATT_Q_W = ATT_HEADS * ATT_HEAD_DIM
ATT_KV_W = ATT_KV_HEADS * ATT_HEAD_DIM
GLA_QK_W = GLA_HEADS * GLA_KEY_DIM
GLA_V_W = GLA_HEADS * GLA_VAL_DIM
MIX_W = ATT_Q_W + GLA_V_W
IN_OFFSETS = (512, 640, 768, 1024, 1280, 1792, 1824)
IN_W = 2336

kernel_name = "hymba_style_attn_gla_convffn_dit_layer"


def rms_norm(x, g):
    xf = x.astype(jnp.float32)
    y = xf * lax.rsqrt(jnp.mean(xf * xf, axis=-1, keepdims=True) + EPS)
    return (y * g.astype(jnp.float32)).astype(x.dtype)


def modulate(h, shift, scale):
    return h * (1 + scale) + shift


def adaln(cond, w, b):
    m = jax.nn.silu(cond) @ w + b
    return jnp.split(m, 6, axis=-1)


def axial_rope_tables(rows):
    row = jnp.broadcast_to(jnp.arange(rows, dtype=jnp.float32)[:, None], (rows, GRID_W)).reshape(-1)
    col = jnp.broadcast_to(jnp.arange(GRID_W, dtype=jnp.float32)[None, :], (rows, GRID_W)).reshape(-1)
    half = ATT_HEAD_DIM // 2
    freqs = ROPE_THETA ** (-jnp.arange(0, half, 2, dtype=jnp.float32) / half)
    ang = jnp.concatenate([row[:, None] * freqs, col[:, None] * freqs], axis=-1)
    return jnp.cos(ang), jnp.sin(ang)


def apply_rope(x, cos, sin):
    xp = x.reshape(x.shape[:-1] + (x.shape[-1] // 2, 2))
    x0, x1 = xp[..., 0], xp[..., 1]
    c = cos[:, None, :].astype(x.dtype)
    s = sin[:, None, :].astype(x.dtype)
    return jnp.stack([x0 * c - x1 * s, x0 * s + x1 * c], axis=-1).reshape(x.shape)


def attend(qg, k, v):
    s = jnp.einsum('bqkgd,bskd->bkgqs', qg, k, preferred_element_type=jnp.float32) * (ATT_HEAD_DIM ** -0.5)
    p = jax.nn.softmax(s, axis=-1)
    return jnp.einsum('bkgqs,bskd->bqkgd', p.astype(v.dtype), v)


def latent_attention(q, k_lat, v_lat, k_ctx, v_ctx):
    B, S = q.shape[0], q.shape[1]
    G = ATT_HEADS // ATT_KV_HEADS
    k_all = jnp.concatenate([k_ctx, k_lat], axis=1)
    v_all = jnp.concatenate([v_ctx, v_lat], axis=1)
    nb = S // Q_BLOCK
    qb = q.reshape(B, nb, Q_BLOCK, ATT_KV_HEADS, G, ATT_HEAD_DIM).transpose(1, 0, 2, 3, 4, 5)
    o = lax.map(lambda qi: attend(qi, k_all, v_all), qb)
    return o.transpose(1, 0, 2, 3, 4, 5).reshape(B, S, ATT_Q_W)


def context_attention(q, k, v):
    B, L = q.shape[0], q.shape[1]
    G = ATT_HEADS // ATT_KV_HEADS
    o = attend(q.reshape(B, L, ATT_KV_HEADS, G, ATT_HEAD_DIM), k, v)
    return o.reshape(B, L, ATT_Q_W)


def gla_chunked(q, k, v, log_a, state0):
    B, L, H, dk = q.shape
    dv = v.shape[-1]
    n = L // GLA_CHUNK

    def chunks(t):
        return t.astype(jnp.float32).reshape(B, n, GLA_CHUNK, H, t.shape[-1]).transpose(0, 3, 1, 2, 4)

    qc, kc, vc, gc = chunks(q), chunks(k), chunks(v), chunks(log_a)
    qc = qc * (dk ** -0.5)
    b = lax.cumsum(gc, axis=3)
    b_last = b[:, :, :, -1:, :]
    q_in = qc * jnp.exp(b)
    k_in = kc * jnp.exp(-b)
    k_up = kc * jnp.exp(b_last - b)
    mask = jnp.tril(jnp.ones((GLA_CHUNK, GLA_CHUNK), dtype=bool))
    att = jnp.where(mask, jnp.einsum('bhntd,bhnsd->bhnts', q_in, k_in), 0.0)
    o_intra = jnp.einsum('bhnts,bhnsv->bhntv', att, vc)
    upd = jnp.einsum('bhnsd,bhnsv->bhndv', k_up, vc)
    decay = jnp.exp(b_last[:, :, :, 0, :])

    def step(S, inp):
        dec, u = inp
        return dec[..., None] * S + u, S

    s_final, s_starts = lax.scan(step, state0.astype(jnp.float32),
                                 (decay.transpose(2, 0, 1, 3), upd.transpose(2, 0, 1, 3, 4)))
    s_starts = s_starts.transpose(1, 2, 0, 3, 4)
    o = o_intra + jnp.einsum('bhntd,bhndv->bhntv', q_in, s_starts)
    return o.transpose(0, 2, 3, 1, 4).reshape(B, L, H, dv), s_final


def log_gate(gl, w2, b2):
    B, L = gl.shape[0], gl.shape[1]
    z = (gl @ w2 + b2).astype(jnp.float32)
    return (jax.nn.log_sigmoid(z) / GLA_GATE_NORMALIZER).reshape(B, L, GLA_HEADS, GLA_KEY_DIM)


def gla_bidirectional(q, k, v, log_f, log_b, s0_f, s0_b):
    flip = lambda t: t[:, ::-1]
    o_f, s_f = gla_chunked(q, k, v, log_f, s0_f)
    o_b, s_b = gla_chunked(flip(q), flip(k), flip(v), flip(log_b), s0_b)
    return o_f + flip(o_b), s_f, s_b


def mixer_inputs(h, w_in_l, g_q_l, g_k_l, w_gf, b_gf, w_gb, b_gb):
    B, L = h.shape[0], h.shape[1]
    z = h @ w_in_l
    aq, ak, av, gq, gk, gv, gl, gr = jnp.split(z, IN_OFFSETS, axis=-1)
    aq = rms_norm(aq.reshape(B, L, ATT_HEADS, ATT_HEAD_DIM), g_q_l)
    ak = rms_norm(ak.reshape(B, L, ATT_KV_HEADS, ATT_HEAD_DIM), g_k_l)
    av = av.reshape(B, L, ATT_KV_HEADS, ATT_HEAD_DIM)
    gq = gq.reshape(B, L, GLA_HEADS, GLA_KEY_DIM)
    gk = gk.reshape(B, L, GLA_HEADS, GLA_KEY_DIM)
    gv = gv.reshape(B, L, GLA_HEADS, GLA_VAL_DIM)
    log_f = log_gate(gl[..., :GLA_GATE_RANK], w_gf, b_gf)
    log_b = log_gate(gl[..., GLA_GATE_RANK:], w_gb, b_gb)
    return aq, ak, av, gq, gk, gv, log_f, log_b, gr


def merge_heads(att_o, gla_o, gr, g_att_l, g_gla_l, w_out_l):
    B, L = att_o.shape[0], att_o.shape[1]
    att_o = rms_norm(att_o, g_att_l)
    gla_o = rms_norm(gla_o.astype(gr.dtype), g_gla_l).reshape(B, L, GLA_V_W) * jax.nn.silu(gr)
    return jnp.concatenate([att_o, gla_o], axis=-1) @ w_out_l


def conv_ffn(h, w_up_l, conv_w_l, conv_b_l, w_down_l):
    u = h @ w_up_l
    u = lax.conv_general_dilated(u, conv_w_l[:, None, :].astype(u.dtype), window_strides=(1,),
                                 padding=((CONV_WIDTH // 2, CONV_WIDTH // 2),),
                                 dimension_numbers=('NWC', 'WIO', 'NWC'),
                                 feature_group_count=u.shape[-1]) + conv_b_l
    val, gate = jnp.split(u, 2, axis=-1)
    return (jax.nn.silu(gate) * val) @ w_down_l


def setup_inputs(seed: int = 0) -> dict:
    key = jax.random.key(seed)
    ks = jax.random.split(key, 24)
    f32 = jnp.float32
    nrm = lambda k, shape, s: jax.random.normal(k, shape, f32) * s
    gain = lambda k, shape: 1.0 + 0.02 * jax.random.normal(k, shape, f32)
    L = DEPTH
    return {
        "x": nrm(ks[0], (BATCH, SEQ, D_MODEL), 1.0),
        "c": nrm(ks[1], (BATCH, D_MODEL), 1.0),
        "ctx": nrm(ks[2], (BATCH, CTX_LEN, D_MODEL), 1.0),
        "c_ctx": nrm(ks[3], (D_MODEL,), 1.0),
        "w_mod": nrm(ks[4], (L, D_MODEL, 6 * D_MODEL), 0.5 * D_MODEL ** -0.5),
        "b_mod": nrm(ks[5], (L, 6 * D_MODEL), 0.02),
        "g_norm1": gain(ks[6], (L, D_MODEL)),
        "w_in": nrm(ks[7], (L, D_MODEL, IN_W), D_MODEL ** -0.5),
        "g_q": gain(ks[8], (L, ATT_HEAD_DIM)),
        "g_k": gain(ks[9], (L, ATT_HEAD_DIM)),
        "w_gate_fwd": nrm(ks[10], (L, GLA_GATE_RANK, GLA_QK_W), GLA_GATE_RANK ** -0.5),
        "b_gate_fwd": nrm(ks[11], (L, GLA_QK_W), 0.1),
        "w_gate_bwd": nrm(ks[12], (L, GLA_GATE_RANK, GLA_QK_W), GLA_GATE_RANK ** -0.5),
        "b_gate_bwd": nrm(ks[13], (L, GLA_QK_W), 0.1),
        "g_att_out": gain(ks[14], (L, ATT_Q_W)),
        "g_gla_out": gain(ks[15], (L, GLA_VAL_DIM)),
        "w_out": nrm(ks[16], (L, MIX_W, D_MODEL), MIX_W ** -0.5),
        "g_norm2": gain(ks[17], (L, D_MODEL)),
        "w_up": nrm(ks[18], (L, D_MODEL, 2 * D_FF), D_MODEL ** -0.5),
        "conv_w": nrm(ks[19], (L, CONV_WIDTH, 2 * D_FF), CONV_WIDTH ** -0.5),
        "conv_b": nrm(ks[20], (L, 2 * D_FF), 0.02),
        "w_down": nrm(ks[21], (L, D_FF, D_MODEL), D_FF ** -0.5),
        "g_final": gain(ks[22], (D_MODEL,)),
    }


def reference(x, c, ctx, c_ctx, w_mod, b_mod, g_norm1, w_in, g_q, g_k, w_gate_fwd, b_gate_fwd,
              w_gate_bwd, b_gate_bwd, g_att_out, g_gla_out, w_out, g_norm2, w_up, conv_w, conv_b,
              w_down, g_final):
    B, S = x.shape[0], x.shape[1]
    rows = S // GRID_W
    cos, sin = axial_rope_tables(rows)
    zero_state = jnp.zeros((B, GLA_HEADS, GLA_KEY_DIM, GLA_VAL_DIM), jnp.float32)

    for i in range(DEPTH):
        sh1, sc1, gt1, sh2, sc2, gt2 = adaln(c, w_mod[i], b_mod[i])
        sh1, sc1, gt1, sh2, sc2, gt2 = (m[:, None, :] for m in (sh1, sc1, gt1, sh2, sc2, gt2))
        csh1, csc1, cgt1, csh2, csc2, cgt2 = adaln(c_ctx, w_mod[i], b_mod[i])
        layer_w = (w_in[i], g_q[i], g_k[i], w_gate_fwd[i], b_gate_fwd[i], w_gate_bwd[i], b_gate_bwd[i])

        hc = modulate(rms_norm(ctx, g_norm1[i]), csh1, csc1)
        cq, ck, cv, cgq, cgk, cgv, clf, clb, cgr = mixer_inputs(hc, *layer_w)
        co_gla, s_ctx_f, s_ctx_b = gla_bidirectional(cgq, cgk, cgv, clf, clb, zero_state, zero_state)

        h = modulate(rms_norm(x, g_norm1[i]), sh1, sc1)
        aq, ak, av, gq, gk, gv, lf, lb, gr = mixer_inputs(h, *layer_w)
        aq = apply_rope(aq, cos, sin)
        ak = apply_rope(ak, cos, sin)
        att_o = latent_attention(aq, ak, av, ck, cv)
        gla_o, _, _ = gla_bidirectional(gq, gk, gv, lf, lb, s_ctx_f, s_ctx_b)
        x = x + gt1 * merge_heads(att_o, gla_o, gr, g_att_out[i], g_gla_out[i], w_out[i])
        h2 = modulate(rms_norm(x, g_norm2[i]), sh2, sc2)
        x = x + gt2 * conv_ffn(h2, w_up[i], conv_w[i], conv_b[i], w_down[i])

        if i < DEPTH - 1:
            catt_o = context_attention(cq, ck, cv)
            ctx = ctx + cgt1 * merge_heads(catt_o, co_gla, cgr, g_att_out[i], g_gla_out[i], w_out[i])
            hc2 = modulate(rms_norm(ctx, g_norm2[i]), csh2, csc2)
            ctx = ctx + cgt2 * conv_ffn(hc2, w_up[i], conv_w[i], conv_b[i], w_down[i])

    return rms_norm(x, g_final)
```

```python
import functools
import math

import numpy as np
import jax
import jax.numpy as jnp
from jax import lax
from jax.experimental import pallas as pl
from jax.experimental.pallas import tpu as pltpu

GRID_W = 64
ATT_HEADS = 8
ATT_KV_HEADS = 2
ATT_HEAD_DIM = 64
ATT_GROUP = ATT_HEADS // ATT_KV_HEADS
GLA_HEADS = 4
GLA_KEY_DIM = 64
GLA_VAL_DIM = 128
GLA_GATE_RANK = 16
GLA_GATE_NORMALIZER = 16.0
GLA_CHUNK = 64
CONV_WIDTH = 3
ROPE_THETA = 10000.0
EPS = 1e-6

ATT_Q_W = ATT_HEADS * ATT_HEAD_DIM
ATT_KV_W = ATT_KV_HEADS * ATT_HEAD_DIM
GLA_QK_W = GLA_HEADS * GLA_KEY_DIM
GLA_V_W = GLA_HEADS * GLA_VAL_DIM
LANES = 128
GATE_PAD_W = LANES
VMEM_LIMIT = 56 * 1024 * 1024

F32 = jnp.float32
BF16 = jnp.bfloat16


def _dot(a, b):
    return jnp.dot(a, b, preferred_element_type=F32)


def _split_bf16(a):
    hi = a.astype(BF16)
    lo = (a - hi.astype(F32)).astype(BF16)
    return hi, lo


def _sigmoid(z):
    return 1.0 / (1.0 + jnp.exp(-z))


def _mod_kernel(c_ref, w_ref, b_ref, o_ref):
    c = c_ref[...]
    a_hi, a_lo = _split_bf16(c * _sigmoid(c))
    w_hi, w_lo = _split_bf16(w_ref[...])
    o_ref[...] = _dot(a_hi, w_hi) + _dot(a_lo, w_hi) + _dot(a_hi, w_lo) + b_ref[...]


def _modulation(cond, w_mod, b_mod):
    d = cond.shape[1]
    n = w_mod.shape[1]
    bn = d
    return pl.pallas_call(
        _mod_kernel,
        grid=(n // bn,),
        in_specs=[pl.BlockSpec((8, d), lambda j: (0, 0)),
                  pl.BlockSpec((d, bn), lambda j: (0, j)),
                  pl.BlockSpec((1, bn), lambda j: (0, j))],
        out_specs=pl.BlockSpec((8, bn), lambda j: (0, j)),
        out_shape=jax.ShapeDtypeStruct((8, n), F32),
        compiler_params=pltpu.CompilerParams(dimension_semantics=("parallel",), vmem_limit_bytes=VMEM_LIMIT),
        name="modulation",
    )(cond, w_mod, b_mod.reshape(1, n))


C_AQ, C_AK, C_AV = 0, ATT_Q_W, ATT_Q_W + ATT_KV_W
C_GQ = C_AV + ATT_KV_W
C_GK = C_GQ + GLA_QK_W
C_GV = C_GK + GLA_QK_W
C_GR = C_GV + GLA_V_W
C_GL = C_GR + GLA_V_W
IN_W_PAD = C_GL + GATE_PAD_W
QK_W = ATT_Q_W + ATT_KV_W


def _inproj_kernel(x_ref, sh_ref, sc_ref, g1_ref, w_ref, bd_ref, gqk_ref, cos_ref, sin_ref, w2_ref, b2_ref,
                   q_ref, kt_ref, v_ref, gq_ref, gk_ref, gv_ref, lf_ref, lb_ref, gr_ref):
    x = x_ref[...]
    tm = x.shape[0]
    xn = x * lax.rsqrt(jnp.mean(x * x, axis=-1, keepdims=True) + EPS)
    h = xn * (g1_ref[...] * (1.0 + sc_ref[...])) + sh_ref[...]
    z = _dot(h.astype(BF16), w_ref[...])

    bd = bd_ref[...]
    sq = (z[:, :QK_W] * z[:, :QK_W]).astype(BF16)
    ss = jnp.concatenate(
        [_dot(sq[:, 0:256], bd), _dot(sq[:, 256:512], bd), _dot(sq[:, 512:640], bd[:128, :128])], axis=-1)
    y = z[:, :QK_W] * lax.rsqrt(ss * (1.0 / ATT_HEAD_DIM) + EPS) * gqk_ref[...]

    lane = lax.broadcasted_iota(jnp.int32, (tm, LANES), 1)
    first_half = (lane % ATT_HEAD_DIM) < (ATT_HEAD_DIM // 2)
    cos = cos_ref[...]
    sin = sin_ref[...]
    roped = []
    for j in range(QK_W // LANES):
        yc = y[:, j * LANES:(j + 1) * LANES]
        partner = jnp.where(first_half, pltpu.roll(yc, LANES - 32, 1), pltpu.roll(yc, 32, 1))
        roped.append(yc * cos + partner * sin)
    q_ref[...] = jnp.concatenate(roped[:4], axis=-1).astype(BF16)
    kt_ref[...] = roped[4].T.astype(BF16)

    v = z[:, C_AV:C_AV + ATT_KV_W]
    tail = jnp.where(lane == ATT_HEAD_DIM, 1.0, 0.0)
    v_ref[0] = jnp.where(lane < ATT_HEAD_DIM, v, tail).astype(BF16)
    v_ref[1] = jnp.where(lane < ATT_HEAD_DIM, pltpu.roll(v, ATT_HEAD_DIM, 1), tail).astype(BF16)

    gq_ref[...] = z[:, C_GQ:C_GQ + GLA_QK_W] * (GLA_KEY_DIM ** -0.5)
    gk_ref[...] = z[:, C_GK:C_GK + GLA_QK_W]
    gv_ref[...] = z[:, C_GV:C_GV + GLA_V_W]
    gr_ref[...] = z[:, C_GR:C_GR + GLA_V_W]

    gl_hi, gl_lo = _split_bf16(z[:, C_GL:C_GL + GATE_PAD_W])
    w2 = w2_ref[...]
    zg = _dot(gl_hi, w2) + _dot(gl_lo, w2) + b2_ref[...]
    lg = -(jnp.maximum(-zg, 0.0) + jnp.log(1.0 + jnp.exp(-jnp.abs(zg)))) * (1.0 / GLA_GATE_NORMALIZER)
    lf_ref[...] = lg[:, :GLA_QK_W]
    lb_ref[...] = lg[:, GLA_QK_W:]


def _inproj(x2d, mod3, row_of_tile, g1, w_in_p, bd, gqk, cos_t, sin_t, w2, b2, *, tm, tiles_per_batch, batch):
    t, d = x2d.shape
    s = tm * tiles_per_batch
    n_tiles = t // tm
    ntab = cos_t.shape[0] // tm

    def row(i):
        return row_of_tile(i // tiles_per_batch)

    const = lambda i: (0, 0)
    tile = lambda i: (i, 0)
    in_specs = [
        pl.BlockSpec((tm, d), tile),
        pl.BlockSpec((None, 1, d), lambda i: (row(i), 0, 0)),
        pl.BlockSpec((None, 1, d), lambda i: (row(i), 0, 1)),
        pl.BlockSpec((1, d), const),
        pl.BlockSpec((d, IN_W_PAD), const),
        pl.BlockSpec((256, 256), const),
        pl.BlockSpec((1, QK_W), const),
        pl.BlockSpec((tm, LANES), lambda i: (i % ntab, 0)),
        pl.BlockSpec((tm, LANES), lambda i: (i % ntab, 0)),
        pl.BlockSpec((GATE_PAD_W, 2 * GLA_QK_W), const),
        pl.BlockSpec((1, 2 * GLA_QK_W), const),
    ]
    out_shape = [
        jax.ShapeDtypeStruct((t, ATT_Q_W), BF16),
        jax.ShapeDtypeStruct((batch, ATT_KV_W, s), BF16),
        jax.ShapeDtypeStruct((batch, ATT_KV_HEADS, s, LANES), BF16),
        jax.ShapeDtypeStruct((t, GLA_QK_W), F32),
        jax.ShapeDtypeStruct((t, GLA_QK_W), F32),
        jax.ShapeDtypeStruct((t, GLA_V_W), F32),
        jax.ShapeDtypeStruct((t, GLA_QK_W), F32),
        jax.ShapeDtypeStruct((t, GLA_QK_W), F32),
        jax.ShapeDtypeStruct((t, GLA_V_W), F32),
    ]
    out_specs = [
        pl.BlockSpec((tm, ATT_Q_W), tile),
        pl.BlockSpec((None, ATT_KV_W, tm), lambda i: (i // tiles_per_batch, 0, i % tiles_per_batch)),
        pl.BlockSpec((None, ATT_KV_HEADS, tm, LANES), lambda i: (i // tiles_per_batch, 0, i % tiles_per_batch, 0)),
        pl.BlockSpec((tm, GLA_QK_W), tile),
        pl.BlockSpec((tm, GLA_QK_W), tile),
        pl.BlockSpec((tm, GLA_V_W), tile),
        pl.BlockSpec((tm, GLA_QK_W), tile),
        pl.BlockSpec((tm, GLA_QK_W), tile),
        pl.BlockSpec((tm, GLA_V_W), tile),
    ]
    return pl.pallas_call(
        _inproj_kernel,
        grid=(n_tiles,),
        in_specs=in_specs,
        out_specs=out_specs,
        out_shape=out_shape,
        compiler_params=pltpu.CompilerParams(dimension_semantics=("parallel",), vmem_limit_bytes=VMEM_LIMIT),
        name="inproj",
    )(x2d, mod3, mod3, g1, w_in_p, bd, gqk, cos_t, sin_t, w2, b2)


GLA_ROWS = 256
GLA_NCH = GLA_ROWS // GLA_CHUNK


def _gla_kernel(q_ref, k_ref, v_ref, g_ref, s0_ref, tri_ref, ones_ref, cind_ref, o_ref, sfin_ref, state_ref, *,
                reverse):
    @pl.when(pl.program_id(1) == 0)
    def _():
        state_ref[...] = s0_ref[...]

    r = GLA_ROWS
    g = g_ref[...]
    g_hi, g_lo = _split_bf16(g)
    tri = tri_ref[...]
    ones_bd = ones_ref[...]
    b = _dot(tri, g_hi) + _dot(tri, g_lo)
    b_tot = _dot(ones_bd, g_hi) + _dot(ones_bd, g_lo)
    gt_hi, gt_lo = _split_bf16(g.T)
    cind = cind_ref[...]
    dec_t = jnp.exp(_dot(gt_hi, cind) + _dot(gt_lo, cind))

    q_in = q_ref[...] * jnp.exp(b)
    k = k_ref[...]
    k_in = (k * jnp.exp(-b)).astype(BF16)
    k_up_t = (k * jnp.exp(b_tot - b)).T
    v = v_ref[...].astype(BF16)

    row = lax.broadcasted_iota(jnp.int32, (r, r), 0)
    col = lax.broadcasted_iota(jnp.int32, (r, r), 1)
    if reverse:
        dist, reach = col - row, (GLA_CHUNK - 1) - row % GLA_CHUNK
    else:
        dist, reach = row - col, row % GLA_CHUNK
    att_mask = dist.astype(jnp.uint32) <= reach.astype(jnp.uint32)
    lane_head = lax.broadcasted_iota(jnp.int32, (r, GLA_QK_W), 1) // GLA_KEY_DIM
    col_chunk = lax.broadcasted_iota(jnp.int32, (GLA_KEY_DIM, r), 1) // GLA_CHUNK

    q_heads = []
    o_intra = []
    upd = []
    for h in range(GLA_HEADS):
        qh = jnp.where(lane_head == h, q_in, 0.0).astype(BF16)
        q_heads.append(qh)
        att = lax.dot_general(qh, k_in, (((1,), (1,)), ((), ())), preferred_element_type=F32)
        att = jnp.where(att_mask, att, 0.0).astype(BF16)
        vh = v[:, h * GLA_VAL_DIM:(h + 1) * GLA_VAL_DIM]
        o_intra.append(_dot(att, vh))
        kt_h = k_up_t[h * GLA_KEY_DIM:(h + 1) * GLA_KEY_DIM, :]
        lhs = jnp.concatenate([jnp.where(col_chunk == c, kt_h, 0.0) for c in range(GLA_NCH)], axis=0)
        upd.append(_dot(lhs.astype(BF16), vh))

    state = state_ref[...]
    order = range(GLA_NCH - 1, -1, -1) if reverse else range(GLA_NCH)
    for c in order:
        rows = slice(c * GLA_CHUNK, (c + 1) * GLA_CHUNK)
        q_stack = jnp.concatenate([qh[rows, :] for qh in q_heads], axis=0)
        inter = _dot(q_stack, state.astype(BF16))
        for h in range(GLA_HEADS):
            o_ref[rows, h * GLA_VAL_DIM:(h + 1) * GLA_VAL_DIM] = (
                o_intra[h][rows, :] + inter[h * GLA_CHUNK:(h + 1) * GLA_CHUNK, :])
        u_c = jnp.concatenate([u[rows, :] for u in upd], axis=0)
        state = dec_t[:, c * GLA_VAL_DIM:(c + 1) * GLA_VAL_DIM] * state + u_c
    state_ref[...] = state
    sfin_ref[...] = state


def _gla(gq, gk, gv, lg, s0, consts, *, batch, reverse):
    t = gq.shape[0]
    ng = t // batch // GLA_ROWS
    tri, ones_bd, cind = consts

    def tile(b, g):
        return (b * ng + (ng - 1 - g if reverse else g), 0)

    const = lambda b, g: (0, 0)
    per_batch = lambda b, g: (b, 0, 0)
    state_w = GLA_HEADS * GLA_KEY_DIM
    return pl.pallas_call(
        functools.partial(_gla_kernel, reverse=reverse),
        grid=(batch, ng),
        in_specs=[pl.BlockSpec((GLA_ROWS, GLA_QK_W), tile),
                  pl.BlockSpec((GLA_ROWS, GLA_QK_W), tile),
                  pl.BlockSpec((GLA_ROWS, GLA_V_W), tile),
                  pl.BlockSpec((GLA_ROWS, GLA_QK_W), tile),
                  pl.BlockSpec((None, state_w, GLA_VAL_DIM), per_batch),
                  pl.BlockSpec((GLA_ROWS, GLA_ROWS), const),
                  pl.BlockSpec((GLA_ROWS, GLA_ROWS), const),
                  pl.BlockSpec((GLA_ROWS, GLA_NCH * GLA_VAL_DIM), const)],
        out_specs=[pl.BlockSpec((GLA_ROWS, GLA_V_W), tile),
                   pl.BlockSpec((None, state_w, GLA_VAL_DIM), per_batch)],
        out_shape=[jax.ShapeDtypeStruct((t, GLA_V_W), F32),
                   jax.ShapeDtypeStruct((batch, state_w, GLA_VAL_DIM), F32)],
        scratch_shapes=[pltpu.VMEM((state_w, GLA_VAL_DIM), F32)],
        compiler_params=pltpu.CompilerParams(dimension_semantics=("arbitrary", "arbitrary"),
                                             vmem_limit_bytes=VMEM_LIMIT),
        name="gla_bwd" if reverse else "gla_fwd",
    )(gq, gk, gv, lg, s0, tri, ones_bd, cind)


def _gla_consts(reverse):
    r = np.arange(GLA_ROWS)
    same = (r[:, None] // GLA_CHUNK) == (r[None, :] // GLA_CHUNK)
    tri = same & ((r[None, :] >= r[:, None]) if reverse else (r[None, :] <= r[:, None]))
    cind = (r[:, None] // GLA_CHUNK) == (np.arange(GLA_NCH * GLA_VAL_DIM)[None, :] // GLA_VAL_DIM)
    as_bf16 = lambda m: jnp.asarray(m.astype(np.float32), dtype=BF16)
    return as_bf16(tri), as_bf16(same), as_bf16(cind)


ATT_TQ = 256
ATT_TK = 768
NEG_BIG = -1e30


def _attn_kernel(q_ref, kt_ref, v_ref, o_ref, qs_ref, m_ref, acc_ref, *, tq, tk, nk):
    q = q_ref[...]
    for h in range(ATT_GROUP):
        qs_ref[h * tq:(h + 1) * tq, :] = q[:, h * ATT_HEAD_DIM:(h + 1) * ATT_HEAD_DIM]
    m_ref[...] = jnp.full(m_ref.shape, NEG_BIG, F32)
    acc_ref[...] = jnp.zeros(acc_ref.shape, F32)

    def body(j, carry):
        ks = pl.multiple_of(j * tk, tk)
        s = _dot(qs_ref[...], kt_ref[:, pl.ds(ks, tk)])
        m_prev = m_ref[...]
        m_new = jnp.maximum(m_prev, jnp.max(s, axis=-1, keepdims=True))
        alpha = jnp.exp2(m_prev - m_new)
        p = jnp.exp2(s - m_new).astype(BF16)
        acc_ref[...] = alpha * acc_ref[...] + _dot(p, v_ref[pl.ds(ks, tk), :])
        m_ref[...] = m_new
        return carry

    lax.fori_loop(0, nk, body, 0)
    acc = acc_ref[...]
    out = acc[:, :ATT_HEAD_DIM] / acc[:, ATT_HEAD_DIM:ATT_HEAD_DIM + 1]
    for h in range(ATT_GROUP):
        o_ref[:, h * ATT_HEAD_DIM:(h + 1) * ATT_HEAD_DIM] = out[h * tq:(h + 1) * tq, :]


def _attention(q, kt, vext, *, batch, seq):
    tq, tk = ATT_TQ, ATT_TK
    klen = kt.shape[-1]
    nk = klen // tk
    nq = seq // tq
    gw = ATT_GROUP * ATT_HEAD_DIM
    return pl.pallas_call(
        functools.partial(_attn_kernel, tq=tq, tk=tk, nk=nk),
        grid=(batch, ATT_KV_HEADS, nq),
        in_specs=[pl.BlockSpec((tq, gw), lambda b, kh, i: (b * nq + i, kh)),
                  pl.BlockSpec((None, ATT_HEAD_DIM, klen), lambda b, kh, i: (b, kh, 0)),
                  pl.BlockSpec((None, None, klen, LANES), lambda b, kh, i: (b, kh, 0, 0))],
        out_specs=pl.BlockSpec((tq, gw), lambda b, kh, i: (b * nq + i, kh)),
        out_shape=jax.ShapeDtypeStruct((batch * seq, ATT_Q_W), F32),
        scratch_shapes=[pltpu.VMEM((ATT_GROUP * tq, ATT_HEAD_DIM), BF16),
                        pltpu.VMEM((ATT_GROUP * tq, 1), F32),
                        pltpu.VMEM((ATT_GROUP * tq, LANES), F32)],
        compiler_params=pltpu.CompilerParams(dimension_semantics=("parallel", "parallel", "parallel"),
                                             vmem_limit_bytes=VMEM_LIMIT),
        name="attention",
    )(q, kt, vext)


def _merge_kernel(att_ref, of_ref, ob_ref, gr_ref, x_ref, gt_ref, ga_ref, gg_ref, w_ref, o_ref):
    a = att_ref[...]
    a = a * lax.rsqrt(jnp.mean(a * a, axis=-1, keepdims=True) + EPS) * ga_ref[...]
    og = of_ref[...] + ob_ref[...]
    gr = gr_ref[...]
    gg = gg_ref[...]
    parts = [a.astype(BF16)]
    for h in range(GLA_HEADS):
        sl = slice(h * GLA_VAL_DIM, (h + 1) * GLA_VAL_DIM)
        oh = og[:, sl]
        oh = oh * lax.rsqrt(jnp.mean(oh * oh, axis=-1, keepdims=True) + EPS) * gg
        grh = gr[:, sl]
        parts.append((oh * (grh * _sigmoid(grh))).astype(BF16))
    mix = jnp.concatenate(parts, axis=-1)
    o_ref[...] = x_ref[...] + gt_ref[...] * _dot(mix, w_ref[...])


def _merge(att_o, o_f, o_b, gr, x2d, mod3, g_att, g_gla, w_out, *, tm, tiles_per_batch):
    t, d = x2d.shape
    tile = lambda i: (i, 0)
    const = lambda i: (0, 0)
    return pl.pallas_call(
        _merge_kernel,
        grid=(t // tm,),
        in_specs=[pl.BlockSpec((tm, ATT_Q_W), tile),
                  pl.BlockSpec((tm, GLA_V_W), tile),
                  pl.BlockSpec((tm, GLA_V_W), tile),
                  pl.BlockSpec((tm, GLA_V_W), tile),
                  pl.BlockSpec((tm, d), tile),
                  pl.BlockSpec((None, 1, d), lambda i: (i // tiles_per_batch, 0, 2)),
                  pl.BlockSpec((1, ATT_Q_W), const),
                  pl.BlockSpec((1, GLA_VAL_DIM), const),
                  pl.BlockSpec((ATT_Q_W + GLA_V_W, d), const)],
        out_specs=pl.BlockSpec((tm, d), tile),
        out_shape=jax.ShapeDtypeStruct((t, d), F32),
        compiler_params=pltpu.CompilerParams(dimension_semantics=("parallel",), vmem_limit_bytes=VMEM_LIMIT),
        name="merge",
    )(att_o, o_f, o_b, gr, x2d, mod3, g_att, g_gla, w_out)


HALO = 16
FFN_BLOCK = 256


def _ffn_kernel(x_ref, xp_ref, xn_ref, sh_ref, sc_ref, gt_ref, g2_ref, gf_ref, wup_ref, cw_ref, cb_ref, wdn_ref,
                o_ref, h_ref, u_ref, acc_ref, *, tm, tiles_per_batch, d_ff):
    i = pl.program_id(0)
    pos = i % tiles_per_batch
    scale = g2_ref[...] * (1.0 + sc_ref[...])
    shift = sh_ref[...]

    def normed(xv):
        return xv * lax.rsqrt(jnp.mean(xv * xv, axis=-1, keepdims=True) + EPS) * scale + shift

    x = x_ref[...]
    keep_prev = jnp.where(pos > 0, 1.0, 0.0)
    keep_next = jnp.where(pos < tiles_per_batch - 1, 1.0, 0.0)
    h_ref[0:HALO, :] = (normed(xp_ref[...]) * keep_prev).astype(BF16)
    h_ref[HALO:HALO + tm, :] = normed(x).astype(BF16)
    h_ref[HALO + tm:, :] = (normed(xn_ref[...]) * keep_next).astype(BF16)

    hb = h_ref[...]
    nb = FFN_BLOCK
    for j in range(d_ff // nb):
        for half, base in enumerate((0, d_ff)):
            cols = slice(base + j * nb, base + (j + 1) * nb)
            u_ref[...] = _dot(hb, wup_ref[:, cols])
            cw = cw_ref[:, cols]
            conv = (u_ref[HALO - 1:HALO - 1 + tm, :] * cw[0:1, :] + u_ref[HALO:HALO + tm, :] * cw[1:2, :]
                    + u_ref[HALO + 1:HALO + 1 + tm, :] * cw[2:3, :] + cb_ref[:, cols])
            if half == 0:
                val = conv
            else:
                act = (conv * _sigmoid(conv) * val).astype(BF16)
        contrib = _dot(act, wdn_ref[j * nb:(j + 1) * nb, :])
        if j == 0:
            acc_ref[...] = contrib
        else:
            acc_ref[...] += contrib

    y = x + gt_ref[...] * acc_ref[...]
    o_ref[...] = y * lax.rsqrt(jnp.mean(y * y, axis=-1, keepdims=True) + EPS) * gf_ref[...]


def _ffn(x1, mod3, g2, g_final, w_up, conv_w, conv_b, w_down, *, tm, tiles_per_batch):
    t, d = x1.shape
    d_ff = w_down.shape[0]
    hb = tm // HALO
    n_halo = t // HALO
    tile = lambda i: (i, 0)
    const = lambda i: (0, 0)
    row = lambda i: i // tiles_per_batch
    single = pl.Buffered(1)
    return pl.pallas_call(
        functools.partial(_ffn_kernel, tm=tm, tiles_per_batch=tiles_per_batch, d_ff=d_ff),
        grid=(t // tm,),
        in_specs=[pl.BlockSpec((tm, d), tile),
                  pl.BlockSpec((HALO, d), lambda i: (jnp.maximum(i * hb - 1, 0), 0)),
                  pl.BlockSpec((HALO, d), lambda i: (jnp.minimum((i + 1) * hb, n_halo - 1), 0)),
                  pl.BlockSpec((None, 1, d), lambda i: (row(i), 0, 3)),
                  pl.BlockSpec((None, 1, d), lambda i: (row(i), 0, 4)),
                  pl.BlockSpec((None, 1, d), lambda i: (row(i), 0, 5)),
                  pl.BlockSpec((1, d), const),
                  pl.BlockSpec((1, d), const),
                  pl.BlockSpec((d, 2 * d_ff), const, pipeline_mode=single),
                  pl.BlockSpec((8, 2 * d_ff), const),
                  pl.BlockSpec((1, 2 * d_ff), const),
                  pl.BlockSpec((d_ff, d), const, pipeline_mode=single)],
        out_specs=pl.BlockSpec((tm, d), tile),
        out_shape=jax.ShapeDtypeStruct((t, d), F32),
        scratch_shapes=[pltpu.VMEM((tm + 2 * HALO, d), BF16),
                        pltpu.VMEM((tm + 2 * HALO, FFN_BLOCK), F32),
                        pltpu.VMEM((tm, d), F32)],
        compiler_params=pltpu.CompilerParams(dimension_semantics=("parallel",), vmem_limit_bytes=VMEM_LIMIT),
        name="ffn",
    )(x1, x1, x1, mod3, mod3, mod3, g2, g_final, w_up, conv_w, conv_b, w_down)


def _head_perm(n_heads):
    j = np.arange(ATT_HEAD_DIM // 2)
    one = np.concatenate([2 * j, 2 * j + 1])
    return np.concatenate([h * ATT_HEAD_DIM + one for h in range(n_heads)])


def _rope_tables(seq):
    rows = seq // GRID_W
    row = jnp.broadcast_to(jnp.arange(rows, dtype=F32)[:, None], (rows, GRID_W)).reshape(-1)
    col = jnp.broadcast_to(jnp.arange(GRID_W, dtype=F32)[None, :], (rows, GRID_W)).reshape(-1)
    half = ATT_HEAD_DIM // 2
    freqs = ROPE_THETA ** (-jnp.arange(0, half, 2, dtype=F32) / half)
    ang = jnp.concatenate([row[:, None] * freqs, col[:, None] * freqs], axis=-1)
    cos, sin = jnp.cos(ang), jnp.sin(ang)
    cos_t = jnp.tile(jnp.concatenate([cos, cos], axis=-1), (1, LANES // ATT_HEAD_DIM))
    sin_t = jnp.tile(jnp.concatenate([-sin, sin], axis=-1), (1, LANES // ATT_HEAD_DIM))
    return cos_t, sin_t


def kernel(x, c, ctx, c_ctx, w_mod, b_mod, g_norm1, w_in, g_q, g_k, w_gate_fwd, b_gate_fwd, w_gate_bwd, b_gate_bwd,
           g_att_out, g_gla_out, w_out, g_norm2, w_up, conv_w, conv_b, w_down, g_final):
    batch, seq, d = x.shape
    ctx_len = ctx.shape[1]
    depth = w_mod.shape[0]
    assert depth == 1 and batch + 1 <= 8
    assert seq % 512 == 0 and ctx_len % GLA_ROWS == 0 and (seq + ctx_len) % ATT_TK == 0

    cond = jnp.zeros((8, d), F32).at[:batch].set(c).at[batch].set(c_ctx)
    mod = _modulation(cond, w_mod[0], b_mod[0])
    mod3 = mod.reshape(8, 1, 6 * d)

    wi = w_in[0]
    o = (ATT_Q_W, ATT_Q_W + ATT_KV_W, ATT_Q_W + 2 * ATT_KV_W)
    o_gq = o[2]
    o_gk = o_gq + GLA_QK_W
    o_gv = o_gk + GLA_QK_W
    o_gl = o_gv + GLA_V_W
    o_gr = o_gl + 2 * GLA_GATE_RANK
    w_in_p = jnp.concatenate([
        wi[:, :o[0]][:, _head_perm(ATT_HEADS)],
        wi[:, o[0]:o[1]][:, _head_perm(ATT_KV_HEADS)],
        wi[:, o[1]:o_gv],
        wi[:, o_gv:o_gl],
        wi[:, o_gr:],
        wi[:, o_gl:o_gr],
        jnp.zeros((d, GATE_PAD_W - 2 * GLA_GATE_RANK), F32)], axis=1).astype(BF16)
    perm1 = _head_perm(1)
    q_scale = (ATT_HEAD_DIM ** -0.5) * math.log2(math.e)
    gqk = jnp.concatenate([jnp.tile(g_q[0][perm1] * q_scale, ATT_HEADS),
                           jnp.tile(g_k[0][perm1], ATT_KV_HEADS)]).reshape(1, QK_W)
    blk = np.arange(256) // ATT_HEAD_DIM
    bd = jnp.asarray((blk[:, None] == blk[None, :]).astype(np.float32), dtype=BF16)
    w2 = jnp.zeros((GATE_PAD_W, 2 * GLA_QK_W), F32)
    w2 = w2.at[:GLA_GATE_RANK, :GLA_QK_W].set(w_gate_fwd[0])
    w2 = w2.at[GLA_GATE_RANK:2 * GLA_GATE_RANK, GLA_QK_W:].set(w_gate_bwd[0]).astype(BF16)
    b2 = jnp.concatenate([b_gate_fwd[0], b_gate_bwd[0]]).reshape(1, 2 * GLA_QK_W)
    g1 = g_norm1[0].reshape(1, d)

    cos_t, sin_t = _rope_tables(seq)
    ones_t = jnp.ones((ctx_len, LANES), F32)
    zeros_t = jnp.zeros((ctx_len, LANES), F32)

    tm = 512
    tpb = seq // tm
    x2d = x.reshape(batch * seq, d)
    ctx2d = ctx.reshape(batch * ctx_len, d)

    shared = (g1, w_in_p, bd, gqk)
    _, ckt, cv, cgq, cgk, cgv, clf, clb, _ = _inproj(
        ctx2d, mod3, lambda b: batch, *shared, ones_t, zeros_t, w2, b2,
        tm=ctx_len, tiles_per_batch=1, batch=batch)
    q, kt, vext, gq, gk, gv, lf, lb, gr = _inproj(
        x2d, mod3, lambda b: b, *shared, cos_t, sin_t, w2, b2, tm=tm, tiles_per_batch=tpb, batch=batch)

    consts_f = _gla_consts(False)
    consts_b = _gla_consts(True)
    zero_state = jnp.zeros((batch, GLA_HEADS * GLA_KEY_DIM, GLA_VAL_DIM), F32)
    _, s_ctx_f = _gla(cgq, cgk, cgv, clf, zero_state, consts_f, batch=batch, reverse=False)
    _, s_ctx_b = _gla(cgq, cgk, cgv, clb, zero_state, consts_b, batch=batch, reverse=True)
    o_f, _ = _gla(gq, gk, gv, lf, s_ctx_f, consts_f, batch=batch, reverse=False)
    o_b, _ = _gla(gq, gk, gv, lb, s_ctx_b, consts_b, batch=batch, reverse=True)

    kt_all = jnp.concatenate([ckt, kt], axis=-1)
    v_all = jnp.concatenate([cv, vext], axis=2)
    att_o = _attention(q, kt_all, v_all, batch=batch, seq=seq)

    x1 = _merge(att_o, o_f, o_b, gr, x2d, mod3, g_att_out[0].reshape(1, ATT_Q_W),
                g_gla_out[0].reshape(1, GLA_VAL_DIM), w_out[0].astype(BF16), tm=tm, tiles_per_batch=tpb)

    cw8 = jnp.zeros((8, conv_w.shape[-1]), F32).at[:CONV_WIDTH].set(conv_w[0])
    out = _ffn(x1, mod3, g_norm2[0].reshape(1, d), g_final.reshape(1, d), w_up[0].astype(BF16), cw8,
               conv_b[0].reshape(1, -1), w_down[0].astype(BF16), tm=tm, tiles_per_batch=tpb)
    return out.reshape(batch, seq, d)
```

```python
import functools
import math

import numpy as np
import jax
import jax.numpy as jnp
from jax import lax
from jax.experimental import pallas as pl
from jax.experimental.pallas import tpu as pltpu

GRID_W = 64
ATT_HEADS = 8
ATT_KV_HEADS = 2
ATT_HEAD_DIM = 64
ATT_GROUP = ATT_HEADS // ATT_KV_HEADS
GLA_HEADS = 4
GLA_KEY_DIM = 64
GLA_VAL_DIM = 128
GLA_GATE_RANK = 16
GLA_GATE_NORMALIZER = 16.0
GLA_CHUNK = 64
CONV_WIDTH = 3
ROPE_THETA = 10000.0
EPS = 1e-6

ATT_Q_W = ATT_HEADS * ATT_HEAD_DIM
ATT_KV_W = ATT_KV_HEADS * ATT_HEAD_DIM
GLA_QK_W = GLA_HEADS * GLA_KEY_DIM
GLA_V_W = GLA_HEADS * GLA_VAL_DIM
LANES = 128
GATE_PAD_W = LANES
VMEM_LIMIT = 56 * 1024 * 1024

F32 = jnp.float32
BF16 = jnp.bfloat16


def _dot(a, b):
    return jnp.dot(a, b, preferred_element_type=F32)


def _split_bf16(a):
    hi = a.astype(BF16)
    lo = (a - hi.astype(F32)).astype(BF16)
    return hi, lo


def _sigmoid(z):
    return 1.0 / (1.0 + jnp.exp(-z))


def _mod_kernel(c_ref, w_ref, b_ref, o_ref):
    c = c_ref[...]
    a_hi, a_lo = _split_bf16(c * _sigmoid(c))
    w_hi, w_lo = _split_bf16(w_ref[...])
    o_ref[...] = _dot(a_hi, w_hi) + _dot(a_lo, w_hi) + _dot(a_hi, w_lo) + b_ref[...]


def _modulation(cond, w_mod, b_mod):
    d = cond.shape[1]
    n = w_mod.shape[1]
    bn = d
    return pl.pallas_call(
        _mod_kernel,
        grid=(n // bn,),
        in_specs=[pl.BlockSpec((8, d), lambda j: (0, 0)),
                  pl.BlockSpec((d, bn), lambda j: (0, j)),
                  pl.BlockSpec((1, bn), lambda j: (0, j))],
        out_specs=pl.BlockSpec((8, bn), lambda j: (0, j)),
        out_shape=jax.ShapeDtypeStruct((8, n), F32),
        compiler_params=pltpu.CompilerParams(dimension_semantics=("parallel",), vmem_limit_bytes=VMEM_LIMIT),
        name="modulation",
    )(cond, w_mod, b_mod.reshape(1, n))


C_AQ, C_AK, C_AV = 0, ATT_Q_W, ATT_Q_W + ATT_KV_W
C_GQ = C_AV + ATT_KV_W
C_GK = C_GQ + GLA_QK_W
C_GV = C_GK + GLA_QK_W
C_GR = C_GV + GLA_V_W
C_GL = C_GR + GLA_V_W
IN_W_PAD = C_GL + GATE_PAD_W
QK_W = ATT_Q_W + ATT_KV_W


def _inproj_kernel(x_ref, sh_ref, sc_ref, g1_ref, w_ref, bd_ref, gqk_ref, cos_ref, sin_ref, w2_ref, b2_ref,
                   q_ref, kt_ref, v_ref, gq_ref, gk_ref, gv_ref, lf_ref, lb_ref, gr_ref):
    x = x_ref[...]
    tm = x.shape[0]
    xn = x * lax.rsqrt(jnp.mean(x * x, axis=-1, keepdims=True) + EPS)
    h = xn * (g1_ref[...] * (1.0 + sc_ref[...])) + sh_ref[...]
    z = _dot(h.astype(BF16), w_ref[...])

    bd = bd_ref[...]
    sq = (z[:, :QK_W] * z[:, :QK_W]).astype(BF16)
    ss = jnp.concatenate(
        [_dot(sq[:, 0:256], bd), _dot(sq[:, 256:512], bd), _dot(sq[:, 512:640], bd[:128, :128])], axis=-1)
    y = z[:, :QK_W] * lax.rsqrt(ss * (1.0 / ATT_HEAD_DIM) + EPS) * gqk_ref[...]

    lane = lax.broadcasted_iota(jnp.int32, (tm, LANES), 1)
    first_half = (lane % ATT_HEAD_DIM) < (ATT_HEAD_DIM // 2)
    cos = cos_ref[...]
    sin = sin_ref[...]
    roped = []
    for j in range(QK_W // LANES):
        yc = y[:, j * LANES:(j + 1) * LANES]
        partner = jnp.where(first_half, pltpu.roll(yc, LANES - 32, 1), pltpu.roll(yc, 32, 1))
        roped.append(yc * cos + partner * sin)
    q_ref[...] = jnp.concatenate(roped[:4], axis=-1).astype(BF16)
    kt_ref[...] = roped[4].T.astype(BF16)

    v = z[:, C_AV:C_AV + ATT_KV_W]
    tail = jnp.where(lane == ATT_HEAD_DIM, 1.0, 0.0)
    v_ref[0] = jnp.where(lane < ATT_HEAD_DIM, v, tail).astype(BF16)
    v_ref[1] = jnp.where(lane < ATT_HEAD_DIM, pltpu.roll(v, ATT_HEAD_DIM, 1), tail).astype(BF16)

    gq_ref[...] = z[:, C_GQ:C_GQ + GLA_QK_W] * (GLA_KEY_DIM ** -0.5)
    gk_ref[...] = z[:, C_GK:C_GK + GLA_QK_W]
    gv_ref[...] = z[:, C_GV:C_GV + GLA_V_W]
    gr_ref[...] = z[:, C_GR:C_GR + GLA_V_W]

    gl_hi, gl_lo = _split_bf16(z[:, C_GL:C_GL + GATE_PAD_W])
    w2 = w2_ref[...]
    zg = _dot(gl_hi, w2) + _dot(gl_lo, w2) + b2_ref[...]
    lg = -(jnp.maximum(-zg, 0.0) + jnp.log(1.0 + jnp.exp(-jnp.abs(zg)))) * (1.0 / GLA_GATE_NORMALIZER)
    lf_ref[...] = lg[:, :GLA_QK_W]
    lb_ref[...] = lg[:, GLA_QK_W:]


def _inproj(x2d, mod3, row_of_tile, g1, w_in_p, bd, gqk, cos_t, sin_t, w2, b2, *, tm, tiles_per_batch, batch):
    t, d = x2d.shape
    s = tm * tiles_per_batch
    n_tiles = t // tm
    ntab = cos_t.shape[0] // tm

    def row(i):
        return row_of_tile(i // tiles_per_batch)

    const = lambda i: (0, 0)
    tile = lambda i: (i, 0)
    in_specs = [
        pl.BlockSpec((tm, d), tile),
        pl.BlockSpec((None, 1, d), lambda i: (row(i), 0, 0)),
        pl.BlockSpec((None, 1, d), lambda i: (row(i), 0, 1)),
        pl.BlockSpec((1, d), const),
        pl.BlockSpec((d, IN_W_PAD), const),
        pl.BlockSpec((256, 256), const),
        pl.BlockSpec((1, QK_W), const),
        pl.BlockSpec((tm, LANES), lambda i: (i % ntab, 0)),
        pl.BlockSpec((tm, LANES), lambda i: (i % ntab, 0)),
        pl.BlockSpec((GATE_PAD_W, 2 * GLA_QK_W), const),
        pl.BlockSpec((1, 2 * GLA_QK_W), const),
    ]
    out_shape = [
        jax.ShapeDtypeStruct((t, ATT_Q_W), BF16),
        jax.ShapeDtypeStruct((batch, ATT_KV_W, s), BF16),
        jax.ShapeDtypeStruct((batch, ATT_KV_HEADS, s, LANES), BF16),
        jax.ShapeDtypeStruct((t, GLA_QK_W), F32),
        jax.ShapeDtypeStruct((t, GLA_QK_W), F32),
        jax.ShapeDtypeStruct((t, GLA_V_W), F32),
        jax.ShapeDtypeStruct((t, GLA_QK_W), F32),
        jax.ShapeDtypeStruct((t, GLA_QK_W), F32),
        jax.ShapeDtypeStruct((t, GLA_V_W), F32),
    ]
    out_specs = [
        pl.BlockSpec((tm, ATT_Q_W), tile),
        pl.BlockSpec((None, ATT_KV_W, tm), lambda i: (i // tiles_per_batch, 0, i % tiles_per_batch)),
        pl.BlockSpec((None, ATT_KV_HEADS, tm, LANES), lambda i: (i // tiles_per_batch, 0, i % tiles_per_batch, 0)),
        pl.BlockSpec((tm, GLA_QK_W), tile),
        pl.BlockSpec((tm, GLA_QK_W), tile),
        pl.BlockSpec((tm, GLA_V_W), tile),
        pl.BlockSpec((tm, GLA_QK_W), tile),
        pl.BlockSpec((tm, GLA_QK_W), tile),
        pl.BlockSpec((tm, GLA_V_W), tile),
    ]
    return pl.pallas_call(
        _inproj_kernel,
        grid=(n_tiles,),
        in_specs=in_specs,
        out_specs=out_specs,
        out_shape=out_shape,
        compiler_params=pltpu.CompilerParams(dimension_semantics=("parallel",), vmem_limit_bytes=VMEM_LIMIT),
        name="inproj",
    )(x2d, mod3, mod3, g1, w_in_p, bd, gqk, cos_t, sin_t, w2, b2)


GLA_ROWS = 256
GLA_NCH = GLA_ROWS // GLA_CHUNK


def _gla_kernel(q_ref, k_ref, v_ref, g_ref, s0_ref, tri_ref, ones_ref, cind_ref, o_ref, sfin_ref, state_ref, *,
                reverse):
    @pl.when(pl.program_id(1) == 0)
    def _():
        state_ref[...] = s0_ref[...]

    r = GLA_ROWS
    g = g_ref[...]
    g_hi, g_lo = _split_bf16(g)
    tri = tri_ref[...]
    ones_bd = ones_ref[...]
    b = _dot(tri, g_hi) + _dot(tri, g_lo)
    b_tot = _dot(ones_bd, g_hi) + _dot(ones_bd, g_lo)
    gt_hi, gt_lo = _split_bf16(g.T)
    cind = cind_ref[...]
    dec_t = jnp.exp(_dot(gt_hi, cind) + _dot(gt_lo, cind))

    q_in = q_ref[...] * jnp.exp(b)
    k = k_ref[...]
    k_in = (k * jnp.exp(-b)).astype(BF16)
    k_up_t = (k * jnp.exp(b_tot - b)).T
    v = v_ref[...].astype(BF16)

    row = lax.broadcasted_iota(jnp.int32, (r, r), 0)
    col = lax.broadcasted_iota(jnp.int32, (r, r), 1)
    if reverse:
        dist, reach = col - row, (GLA_CHUNK - 1) - row % GLA_CHUNK
    else:
        dist, reach = row - col, row % GLA_CHUNK
    att_mask = dist.astype(jnp.uint32) <= reach.astype(jnp.uint32)
    lane_head = lax.broadcasted_iota(jnp.int32, (r, GLA_QK_W), 1) // GLA_KEY_DIM
    col_chunk = lax.broadcasted_iota(jnp.int32, (GLA_KEY_DIM, r), 1) // GLA_CHUNK

    q_heads = []
    o_intra = []
    upd = []
    for h in range(GLA_HEADS):
        qh = jnp.where(lane_head == h, q_in, 0.0).astype(BF16)
        q_heads.append(qh)
        att = lax.dot_general(qh, k_in, (((1,), (1,)), ((), ())), preferred_element_type=F32)
        att = jnp.where(att_mask, att, 0.0).astype(BF16)
        vh = v[:, h * GLA_VAL_DIM:(h + 1) * GLA_VAL_DIM]
        o_intra.append(_dot(att, vh))
        kt_h = k_up_t[h * GLA_KEY_DIM:(h + 1) * GLA_KEY_DIM, :]
        lhs = jnp.concatenate([jnp.where(col_chunk == c, kt_h, 0.0) for c in range(GLA_NCH)], axis=0)
        upd.append(_dot(lhs.astype(BF16), vh))

    state = state_ref[...]
    order = range(GLA_NCH - 1, -1, -1) if reverse else range(GLA_NCH)
    for c in order:
        rows = slice(c * GLA_CHUNK, (c + 1) * GLA_CHUNK)
        q_stack = jnp.concatenate([qh[rows, :] for qh in q_heads], axis=0)
        inter = _dot(q_stack, state.astype(BF16))
        for h in range(GLA_HEADS):
            o_ref[rows, h * GLA_VAL_DIM:(h + 1) * GLA_VAL_DIM] = (
                o_intra[h][rows, :] + inter[h * GLA_CHUNK:(h + 1) * GLA_CHUNK, :])
        u_c = jnp.concatenate([u[rows, :] for u in upd], axis=0)
        state = dec_t[:, c * GLA_VAL_DIM:(c + 1) * GLA_VAL_DIM] * state + u_c
    state_ref[...] = state
    sfin_ref[...] = state


def _gla(gq, gk, gv, lg, s0, consts, *, batch, reverse):
    t = gq.shape[0]
    ng = t // batch // GLA_ROWS
    tri, ones_bd, cind = consts

    def tile(b, g):
        return (b * ng + (ng - 1 - g if reverse else g), 0)

    const = lambda b, g: (0, 0)
    per_batch = lambda b, g: (b, 0, 0)
    state_w = GLA_HEADS * GLA_KEY_DIM
    return pl.pallas_call(
        functools.partial(_gla_kernel, reverse=reverse),
        grid=(batch, ng),
        in_specs=[pl.BlockSpec((GLA_ROWS, GLA_QK_W), tile),
                  pl.BlockSpec((GLA_ROWS, GLA_QK_W), tile),
                  pl.BlockSpec((GLA_ROWS, GLA_V_W), tile),
                  pl.BlockSpec((GLA_ROWS, GLA_QK_W), tile),
                  pl.BlockSpec((None, state_w, GLA_VAL_DIM), per_batch),
                  pl.BlockSpec((GLA_ROWS, GLA_ROWS), const),
                  pl.BlockSpec((GLA_ROWS, GLA_ROWS), const),
                  pl.BlockSpec((GLA_ROWS, GLA_NCH * GLA_VAL_DIM), const)],
        out_specs=[pl.BlockSpec((GLA_ROWS, GLA_V_W), tile),
                   pl.BlockSpec((None, state_w, GLA_VAL_DIM), per_batch)],
        out_shape=[jax.ShapeDtypeStruct((t, GLA_V_W), F32),
                   jax.ShapeDtypeStruct((batch, state_w, GLA_VAL_DIM), F32)],
        scratch_shapes=[pltpu.VMEM((state_w, GLA_VAL_DIM), F32)],
        compiler_params=pltpu.CompilerParams(dimension_semantics=("arbitrary", "arbitrary"),
                                             vmem_limit_bytes=VMEM_LIMIT),
        name="gla_bwd" if reverse else "gla_fwd",
    )(gq, gk, gv, lg, s0, tri, ones_bd, cind)


def _gla_consts(reverse):
    r = np.arange(GLA_ROWS)
    same = (r[:, None] // GLA_CHUNK) == (r[None, :] // GLA_CHUNK)
    tri = same & ((r[None, :] >= r[:, None]) if reverse else (r[None, :] <= r[:, None]))
    cind = (r[:, None] // GLA_CHUNK) == (np.arange(GLA_NCH * GLA_VAL_DIM)[None, :] // GLA_VAL_DIM)
    as_bf16 = lambda m: jnp.asarray(m.astype(np.float32), dtype=BF16)
    return as_bf16(tri), as_bf16(same), as_bf16(cind)


ATT_TQ = 256
ATT_TK = 768
NEG_BIG = -1e30
MAX_UNSHIFTED_SCORE = 64.0


def _attn_kernel(q_ref, kt_ref, v_ref, o_ref, qs_ref, m_ref, acc_ref, *, tq, tk, nk, online):
    q = q_ref[...]
    for h in range(ATT_GROUP):
        qs_ref[h * tq:(h + 1) * tq, :] = q[:, h * ATT_HEAD_DIM:(h + 1) * ATT_HEAD_DIM]
    acc_ref[...] = jnp.zeros(acc_ref.shape, F32)
    if online:
        m_ref[...] = jnp.full(m_ref.shape, NEG_BIG, F32)

    def body(j, carry):
        ks = pl.multiple_of(j * tk, tk)
        s = _dot(qs_ref[...], kt_ref[:, pl.ds(ks, tk)])
        if online:
            m_prev = m_ref[...]
            m_new = jnp.maximum(m_prev, jnp.max(s, axis=-1, keepdims=True))
            p = jnp.exp2(s - m_new).astype(BF16)
            acc_ref[...] = jnp.exp2(m_prev - m_new) * acc_ref[...] + _dot(p, v_ref[pl.ds(ks, tk), :])
            m_ref[...] = m_new
        else:
            acc_ref[...] += _dot(jnp.exp2(s).astype(BF16), v_ref[pl.ds(ks, tk), :])
        return carry

    lax.fori_loop(0, nk, body, 0)
    acc = acc_ref[...]
    out = acc[:, :ATT_HEAD_DIM] / acc[:, ATT_HEAD_DIM:ATT_HEAD_DIM + 1]
    for h in range(ATT_GROUP):
        o_ref[:, h * ATT_HEAD_DIM:(h + 1) * ATT_HEAD_DIM] = out[h * tq:(h + 1) * tq, :]


def _attention_call(q, kt, vext, *, batch, seq, online):
    tq, tk = ATT_TQ, ATT_TK
    klen = kt.shape[-1]
    nk = klen // tk
    nq = seq // tq
    gw = ATT_GROUP * ATT_HEAD_DIM
    return pl.pallas_call(
        functools.partial(_attn_kernel, tq=tq, tk=tk, nk=nk, online=online),
        grid=(batch, ATT_KV_HEADS, nq),
        in_specs=[pl.BlockSpec((tq, gw), lambda b, kh, i: (b * nq + i, kh)),
                  pl.BlockSpec((None, ATT_HEAD_DIM, klen), lambda b, kh, i: (b, kh, 0)),
                  pl.BlockSpec((None, None, klen, LANES), lambda b, kh, i: (b, kh, 0, 0))],
        out_specs=pl.BlockSpec((tq, gw), lambda b, kh, i: (b * nq + i, kh)),
        out_shape=jax.ShapeDtypeStruct((batch * seq, ATT_Q_W), F32),
        scratch_shapes=[pltpu.VMEM((ATT_GROUP * tq, ATT_HEAD_DIM), BF16),
                        pltpu.VMEM((ATT_GROUP * tq, 1), F32),
                        pltpu.VMEM((ATT_GROUP * tq, LANES), F32)],
        compiler_params=pltpu.CompilerParams(dimension_semantics=("parallel", "parallel", "parallel"),
                                             vmem_limit_bytes=VMEM_LIMIT),
        name="attention_online" if online else "attention",
    )(q, kt, vext)


def _attention(q, kt, vext, score_bound, *, batch, seq):
    call = functools.partial(_attention_call, batch=batch, seq=seq)
    return lax.cond(score_bound <= MAX_UNSHIFTED_SCORE,
                    functools.partial(call, online=False), functools.partial(call, online=True), q, kt, vext)


def _merge_kernel(att_ref, of_ref, ob_ref, gr_ref, x_ref, gt_ref, ga_ref, gg_ref, w_ref, o_ref):
    a = att_ref[...]
    a = a * lax.rsqrt(jnp.mean(a * a, axis=-1, keepdims=True) + EPS) * ga_ref[...]
    og = of_ref[...] + ob_ref[...]
    gr = gr_ref[...]
    gg = gg_ref[...]
    parts = [a.astype(BF16)]
    for h in range(GLA_HEADS):
        sl = slice(h * GLA_VAL_DIM, (h + 1) * GLA_VAL_DIM)
        oh = og[:, sl]
        oh = oh * lax.rsqrt(jnp.mean(oh * oh, axis=-1, keepdims=True) + EPS) * gg
        grh = gr[:, sl]
        parts.append((oh * (grh * _sigmoid(grh))).astype(BF16))
    mix = jnp.concatenate(parts, axis=-1)
    o_ref[...] = x_ref[...] + gt_ref[...] * _dot(mix, w_ref[...])


def _merge(att_o, o_f, o_b, gr, x2d, mod3, g_att, g_gla, w_out, *, tm, tiles_per_batch):
    t, d = x2d.shape
    tile = lambda i: (i, 0)
    const = lambda i: (0, 0)
    return pl.pallas_call(
        _merge_kernel,
        grid=(t // tm,),
        in_specs=[pl.BlockSpec((tm, ATT_Q_W), tile),
                  pl.BlockSpec((tm, GLA_V_W), tile),
                  pl.BlockSpec((tm, GLA_V_W), tile),
                  pl.BlockSpec((tm, GLA_V_W), tile),
                  pl.BlockSpec((tm, d), tile),
                  pl.BlockSpec((None, 1, d), lambda i: (i // tiles_per_batch, 0, 2)),
                  pl.BlockSpec((1, ATT_Q_W), const),
                  pl.BlockSpec((1, GLA_VAL_DIM), const),
                  pl.BlockSpec((ATT_Q_W + GLA_V_W, d), const)],
        out_specs=pl.BlockSpec((tm, d), tile),
        out_shape=jax.ShapeDtypeStruct((t, d), F32),
        compiler_params=pltpu.CompilerParams(dimension_semantics=("parallel",), vmem_limit_bytes=VMEM_LIMIT),
        name="merge",
    )(att_o, o_f, o_b, gr, x2d, mod3, g_att, g_gla, w_out)


HALO = 16
FFN_BLOCK = 256


def _ffn_kernel(x_ref, xp_ref, xn_ref, sh_ref, sc_ref, gt_ref, g2_ref, gf_ref, wup_ref, cw_ref, cb_ref, wdn_ref,
                o_ref, h_ref, u_ref, acc_ref, *, tm, tiles_per_batch, d_ff):
    i = pl.program_id(0)
    pos = i % tiles_per_batch
    scale = g2_ref[...] * (1.0 + sc_ref[...])
    shift = sh_ref[...]

    def normed(xv):
        return xv * lax.rsqrt(jnp.mean(xv * xv, axis=-1, keepdims=True) + EPS) * scale + shift

    x = x_ref[...]
    keep_prev = jnp.where(pos > 0, 1.0, 0.0)
    keep_next = jnp.where(pos < tiles_per_batch - 1, 1.0, 0.0)
    h_ref[0:HALO, :] = (normed(xp_ref[...]) * keep_prev).astype(BF16)
    h_ref[HALO:HALO + tm, :] = normed(x).astype(BF16)
    h_ref[HALO + tm:, :] = (normed(xn_ref[...]) * keep_next).astype(BF16)

    hb = h_ref[...]
    nb = FFN_BLOCK
    for j in range(d_ff // nb):
        for half, base in enumerate((0, d_ff)):
            cols = slice(base + j * nb, base + (j + 1) * nb)
            u_ref[...] = _dot(hb, wup_ref[:, cols])
            cw = cw_ref[:, cols]
            conv = (u_ref[HALO - 1:HALO - 1 + tm, :] * cw[0:1, :] + u_ref[HALO:HALO + tm, :] * cw[1:2, :]
                    + u_ref[HALO + 1:HALO + 1 + tm, :] * cw[2:3, :] + cb_ref[:, cols])
            if half == 0:
                val = conv
            else:
                act = (conv * _sigmoid(conv) * val).astype(BF16)
        contrib = _dot(act, wdn_ref[j * nb:(j + 1) * nb, :])
        if j == 0:
            acc_ref[...] = contrib
        else:
            acc_ref[...] += contrib

    y = x + gt_ref[...] * acc_ref[...]
    o_ref[...] = y * lax.rsqrt(jnp.mean(y * y, axis=-1, keepdims=True) + EPS) * gf_ref[...]


def _ffn(x1, mod3, g2, g_final, w_up, conv_w, conv_b, w_down, *, tm, tiles_per_batch):
    t, d = x1.shape
    d_ff = w_down.shape[0]
    hb = tm // HALO
    n_halo = t // HALO
    tile = lambda i: (i, 0)
    const = lambda i: (0, 0)
    row = lambda i: i // tiles_per_batch
    single = pl.Buffered(1)
    return pl.pallas_call(
        functools.partial(_ffn_kernel, tm=tm, tiles_per_batch=tiles_per_batch, d_ff=d_ff),
        grid=(t // tm,),
        in_specs=[pl.BlockSpec((tm, d), tile),
                  pl.BlockSpec((HALO, d), lambda i: (jnp.maximum(i * hb - 1, 0), 0)),
                  pl.BlockSpec((HALO, d), lambda i: (jnp.minimum((i + 1) * hb, n_halo - 1), 0)),
                  pl.BlockSpec((None, 1, d), lambda i: (row(i), 0, 3)),
                  pl.BlockSpec((None, 1, d), lambda i: (row(i), 0, 4)),
                  pl.BlockSpec((None, 1, d), lambda i: (row(i), 0, 5)),
                  pl.BlockSpec((1, d), const),
                  pl.BlockSpec((1, d), const),
                  pl.BlockSpec((d, 2 * d_ff), const, pipeline_mode=single),
                  pl.BlockSpec((8, 2 * d_ff), const),
                  pl.BlockSpec((1, 2 * d_ff), const),
                  pl.BlockSpec((d_ff, d), const, pipeline_mode=single)],
        out_specs=pl.BlockSpec((tm, d), tile),
        out_shape=jax.ShapeDtypeStruct((t, d), F32),
        scratch_shapes=[pltpu.VMEM((tm + 2 * HALO, d), BF16),
                        pltpu.VMEM((tm + 2 * HALO, FFN_BLOCK), F32),
                        pltpu.VMEM((tm, d), F32)],
        compiler_params=pltpu.CompilerParams(dimension_semantics=("parallel",), vmem_limit_bytes=VMEM_LIMIT),
        name="ffn",
    )(x1, x1, x1, mod3, mod3, mod3, g2, g_final, w_up, conv_w, conv_b, w_down)


def _head_perm(n_heads):
    j = np.arange(ATT_HEAD_DIM // 2)
    one = np.concatenate([2 * j, 2 * j + 1])
    return np.concatenate([h * ATT_HEAD_DIM + one for h in range(n_heads)])


def _rope_tables(seq):
    rows = seq // GRID_W
    row = jnp.broadcast_to(jnp.arange(rows, dtype=F32)[:, None], (rows, GRID_W)).reshape(-1)
    col = jnp.broadcast_to(jnp.arange(GRID_W, dtype=F32)[None, :], (rows, GRID_W)).reshape(-1)
    half = ATT_HEAD_DIM // 2
    freqs = ROPE_THETA ** (-jnp.arange(0, half, 2, dtype=F32) / half)
    ang = jnp.concatenate([row[:, None] * freqs, col[:, None] * freqs], axis=-1)
    cos, sin = jnp.cos(ang), jnp.sin(ang)
    cos_t = jnp.tile(jnp.concatenate([cos, cos], axis=-1), (1, LANES // ATT_HEAD_DIM))
    sin_t = jnp.tile(jnp.concatenate([-sin, sin], axis=-1), (1, LANES // ATT_HEAD_DIM))
    return cos_t, sin_t


def kernel(x, c, ctx, c_ctx, w_mod, b_mod, g_norm1, w_in, g_q, g_k, w_gate_fwd, b_gate_fwd, w_gate_bwd, b_gate_bwd,
           g_att_out, g_gla_out, w_out, g_norm2, w_up, conv_w, conv_b, w_down, g_final):
    batch, seq, d = x.shape
    ctx_len = ctx.shape[1]
    depth = w_mod.shape[0]
    assert depth == 1 and batch + 1 <= 8
    assert seq % 512 == 0 and ctx_len % GLA_ROWS == 0 and (seq + ctx_len) % ATT_TK == 0

    cond = jnp.zeros((8, d), F32).at[:batch].set(c).at[batch].set(c_ctx)
    mod = _modulation(cond, w_mod[0], b_mod[0])
    mod3 = mod.reshape(8, 1, 6 * d)

    wi = w_in[0]
    o = (ATT_Q_W, ATT_Q_W + ATT_KV_W, ATT_Q_W + 2 * ATT_KV_W)
    o_gq = o[2]
    o_gk = o_gq + GLA_QK_W
    o_gv = o_gk + GLA_QK_W
    o_gl = o_gv + GLA_V_W
    o_gr = o_gl + 2 * GLA_GATE_RANK
    w_in_p = jnp.concatenate([
        wi[:, :o[0]][:, _head_perm(ATT_HEADS)],
        wi[:, o[0]:o[1]][:, _head_perm(ATT_KV_HEADS)],
        wi[:, o[1]:o_gv],
        wi[:, o_gv:o_gl],
        wi[:, o_gr:],
        wi[:, o_gl:o_gr],
        jnp.zeros((d, GATE_PAD_W - 2 * GLA_GATE_RANK), F32)], axis=1).astype(BF16)
    perm1 = _head_perm(1)
    q_scale = (ATT_HEAD_DIM ** -0.5) * math.log2(math.e)
    gqk = jnp.concatenate([jnp.tile(g_q[0][perm1] * q_scale, ATT_HEADS),
                           jnp.tile(g_k[0][perm1], ATT_KV_HEADS)]).reshape(1, QK_W)
    blk = np.arange(256) // ATT_HEAD_DIM
    bd = jnp.asarray((blk[:, None] == blk[None, :]).astype(np.float32), dtype=BF16)
    w2 = jnp.zeros((GATE_PAD_W, 2 * GLA_QK_W), F32)
    w2 = w2.at[:GLA_GATE_RANK, :GLA_QK_W].set(w_gate_fwd[0])
    w2 = w2.at[GLA_GATE_RANK:2 * GLA_GATE_RANK, GLA_QK_W:].set(w_gate_bwd[0]).astype(BF16)
    b2 = jnp.concatenate([b_gate_fwd[0], b_gate_bwd[0]]).reshape(1, 2 * GLA_QK_W)
    g1 = g_norm1[0].reshape(1, d)

    cos_t, sin_t = _rope_tables(seq)
    ones_t = jnp.ones((ctx_len, LANES), F32)
    zeros_t = jnp.zeros((ctx_len, LANES), F32)

    tm = 512
    tpb = seq // tm
    x2d = x.reshape(batch * seq, d)
    ctx2d = ctx.reshape(batch * ctx_len, d)

    shared = (g1, w_in_p, bd, gqk)
    _, ckt, cv, cgq, cgk, cgv, clf, clb, _ = _inproj(
        ctx2d, mod3, lambda b: batch, *shared, ones_t, zeros_t, w2, b2,
        tm=ctx_len, tiles_per_batch=1, batch=batch)
    q, kt, vext, gq, gk, gv, lf, lb, gr = _inproj(
        x2d, mod3, lambda b: b, *shared, cos_t, sin_t, w2, b2, tm=tm, tiles_per_batch=tpb, batch=batch)

    consts_f = _gla_consts(False)
    consts_b = _gla_consts(True)
    zero_state = jnp.zeros((batch, GLA_HEADS * GLA_KEY_DIM, GLA_VAL_DIM), F32)
    _, s_ctx_f = _gla(cgq, cgk, cgv, clf, zero_state, consts_f, batch=batch, reverse=False)
    _, s_ctx_b = _gla(cgq, cgk, cgv, clb, zero_state, consts_b, batch=batch, reverse=True)
    o_f, _ = _gla(gq, gk, gv, lf, s_ctx_f, consts_f, batch=batch, reverse=False)
    o_b, _ = _gla(gq, gk, gv, lb, s_ctx_b, consts_b, batch=batch, reverse=True)

    kt_all = jnp.concatenate([ckt, kt], axis=-1)
    v_all = jnp.concatenate([cv, vext], axis=2)
    score_bound = ATT_HEAD_DIM * jnp.max(jnp.abs(gqk[0, :ATT_Q_W])) * jnp.max(jnp.abs(gqk[0, ATT_Q_W:]))
    att_o = _attention(q, kt_all, v_all, score_bound, batch=batch, seq=seq)

    x1 = _merge(att_o, o_f, o_b, gr, x2d, mod3, g_att_out[0].reshape(1, ATT_Q_W),
                g_gla_out[0].reshape(1, GLA_VAL_DIM), w_out[0].astype(BF16), tm=tm, tiles_per_batch=tpb)

    cw8 = jnp.zeros((8, conv_w.shape[-1]), F32).at[:CONV_WIDTH].set(conv_w[0])
    out = _ffn(x1, mod3, g_norm2[0].reshape(1, d), g_final.reshape(1, d), w_up[0].astype(BF16), cw8,
               conv_b[0].reshape(1, -1), w_down[0].astype(BF16), tm=tm, tiles_per_batch=tpb)
    return out.reshape(batch, seq, d)
```

```python
import functools
import math

import numpy as np
import jax
import jax.numpy as jnp
from jax import lax
from jax.experimental import pallas as pl
from jax.experimental.pallas import tpu as pltpu

GRID_W = 64
ATT_HEADS = 8
ATT_KV_HEADS = 2
ATT_HEAD_DIM = 64
ATT_GROUP = ATT_HEADS // ATT_KV_HEADS
GLA_HEADS = 4
GLA_KEY_DIM = 64
GLA_VAL_DIM = 128
GLA_GATE_RANK = 16
GLA_GATE_NORMALIZER = 16.0
GLA_CHUNK = 64
CONV_WIDTH = 3
ROPE_THETA = 10000.0
EPS = 1e-6

ATT_Q_W = ATT_HEADS * ATT_HEAD_DIM
ATT_KV_W = ATT_KV_HEADS * ATT_HEAD_DIM
GLA_QK_W = GLA_HEADS * GLA_KEY_DIM
GLA_V_W = GLA_HEADS * GLA_VAL_DIM
LANES = 128
GATE_PAD_W = LANES
VMEM_LIMIT = 56 * 1024 * 1024

F32 = jnp.float32
BF16 = jnp.bfloat16


def _dot(a, b):
    return jnp.dot(a, b, preferred_element_type=F32)


def _split_bf16(a):
    hi = a.astype(BF16)
    lo = (a - hi.astype(F32)).astype(BF16)
    return hi, lo


def _sigmoid(z):
    return 1.0 / (1.0 + jnp.exp(-z))


def _mod_kernel(c_ref, w_ref, b_ref, o_ref):
    c = c_ref[...]
    a_hi, a_lo = _split_bf16(c * _sigmoid(c))
    w_hi, w_lo = _split_bf16(w_ref[...])
    o_ref[...] = _dot(a_hi, w_hi) + _dot(a_lo, w_hi) + _dot(a_hi, w_lo) + b_ref[...]


def _modulation(cond, w_mod, b_mod):
    d = cond.shape[1]
    n = w_mod.shape[1]
    bn = d
    return pl.pallas_call(
        _mod_kernel,
        grid=(n // bn,),
        in_specs=[pl.BlockSpec((8, d), lambda j: (0, 0)),
                  pl.BlockSpec((d, bn), lambda j: (0, j)),
                  pl.BlockSpec((1, bn), lambda j: (0, j))],
        out_specs=pl.BlockSpec((8, bn), lambda j: (0, j)),
        out_shape=jax.ShapeDtypeStruct((8, n), F32),
        compiler_params=pltpu.CompilerParams(dimension_semantics=("parallel",), vmem_limit_bytes=VMEM_LIMIT),
        name="modulation",
    )(cond, w_mod, b_mod.reshape(1, n))


C_AQ, C_AK, C_AV = 0, ATT_Q_W, ATT_Q_W + ATT_KV_W
C_GQ = C_AV + ATT_KV_W
C_GK = C_GQ + GLA_QK_W
C_GV = C_GK + GLA_QK_W
C_GR = C_GV + GLA_V_W
C_GL = C_GR + GLA_V_W
IN_W_PAD = C_GL + GATE_PAD_W
QK_W = ATT_Q_W + ATT_KV_W


def _inproj_kernel(x_ref, sh_ref, sc_ref, g1_ref, w_ref, bd_ref, gqk_ref, cos_ref, sin_ref, w2_ref, b2_ref,
                   q_ref, kt_ref, v_ref, gq_ref, gk_ref, gv_ref, lf_ref, lb_ref, gr_ref):
    x = x_ref[...]
    tm = x.shape[0]
    xn = x * lax.rsqrt(jnp.mean(x * x, axis=-1, keepdims=True) + EPS)
    h = xn * (g1_ref[...] * (1.0 + sc_ref[...])) + sh_ref[...]
    z = _dot(h.astype(BF16), w_ref[...])

    bd = bd_ref[...]
    sq = (z[:, :QK_W] * z[:, :QK_W]).astype(BF16)
    ss = jnp.concatenate(
        [_dot(sq[:, 0:256], bd), _dot(sq[:, 256:512], bd), _dot(sq[:, 512:640], bd[:128, :128])], axis=-1)
    y = z[:, :QK_W] * lax.rsqrt(ss * (1.0 / ATT_HEAD_DIM) + EPS) * gqk_ref[...]

    lane = lax.broadcasted_iota(jnp.int32, (tm, LANES), 1)
    first_half = (lane % ATT_HEAD_DIM) < (ATT_HEAD_DIM // 2)
    cos = cos_ref[...]
    sin = sin_ref[...]
    roped = []
    for j in range(QK_W // LANES):
        yc = y[:, j * LANES:(j + 1) * LANES]
        partner = jnp.where(first_half, pltpu.roll(yc, LANES - 32, 1), pltpu.roll(yc, 32, 1))
        roped.append(yc * cos + partner * sin)
    q_ref[...] = jnp.concatenate(roped[:4], axis=-1).astype(BF16)
    kt_ref[...] = roped[4].T.astype(BF16)

    v = z[:, C_AV:C_AV + ATT_KV_W]
    tail = jnp.where(lane == ATT_HEAD_DIM, 1.0, 0.0)
    v_ref[0] = jnp.where(lane < ATT_HEAD_DIM, v, tail).astype(BF16)
    v_ref[1] = jnp.where(lane < ATT_HEAD_DIM, pltpu.roll(v, ATT_HEAD_DIM, 1), tail).astype(BF16)

    gq_ref[...] = z[:, C_GQ:C_GQ + GLA_QK_W] * (GLA_KEY_DIM ** -0.5)
    gk_ref[...] = z[:, C_GK:C_GK + GLA_QK_W]
    gv_ref[...] = z[:, C_GV:C_GV + GLA_V_W]
    gr_ref[...] = z[:, C_GR:C_GR + GLA_V_W]

    gl_hi, gl_lo = _split_bf16(z[:, C_GL:C_GL + GATE_PAD_W])
    w2 = w2_ref[...]
    zg = _dot(gl_hi, w2) + _dot(gl_lo, w2) + b2_ref[...]
    lg = -(jnp.maximum(-zg, 0.0) + jnp.log(1.0 + jnp.exp(-jnp.abs(zg)))) * (1.0 / GLA_GATE_NORMALIZER)
    lf_ref[...] = lg[:, :GLA_QK_W]
    lb_ref[...] = lg[:, GLA_QK_W:]


def _inproj(x2d, mod3, row_of_tile, g1, w_in_p, bd, gqk, cos_t, sin_t, w2, b2, *, tm, tiles_per_batch, batch):
    t, d = x2d.shape
    s = tm * tiles_per_batch
    n_tiles = t // tm
    ntab = cos_t.shape[0] // tm

    def row(i):
        return row_of_tile(i // tiles_per_batch)

    const = lambda i: (0, 0)
    tile = lambda i: (i, 0)
    in_specs = [
        pl.BlockSpec((tm, d), tile),
        pl.BlockSpec((None, 1, d), lambda i: (row(i), 0, 0)),
        pl.BlockSpec((None, 1, d), lambda i: (row(i), 0, 1)),
        pl.BlockSpec((1, d), const),
        pl.BlockSpec((d, IN_W_PAD), const),
        pl.BlockSpec((256, 256), const),
        pl.BlockSpec((1, QK_W), const),
        pl.BlockSpec((tm, LANES), lambda i: (i % ntab, 0)),
        pl.BlockSpec((tm, LANES), lambda i: (i % ntab, 0)),
        pl.BlockSpec((GATE_PAD_W, 2 * GLA_QK_W), const),
        pl.BlockSpec((1, 2 * GLA_QK_W), const),
    ]
    out_shape = [
        jax.ShapeDtypeStruct((t, ATT_Q_W), BF16),
        jax.ShapeDtypeStruct((batch, ATT_KV_W, s), BF16),
        jax.ShapeDtypeStruct((batch, ATT_KV_HEADS, s, LANES), BF16),
        jax.ShapeDtypeStruct((t, GLA_QK_W), F32),
        jax.ShapeDtypeStruct((t, GLA_QK_W), F32),
        jax.ShapeDtypeStruct((t, GLA_V_W), F32),
        jax.ShapeDtypeStruct((t, GLA_QK_W), F32),
        jax.ShapeDtypeStruct((t, GLA_QK_W), F32),
        jax.ShapeDtypeStruct((t, GLA_V_W), F32),
    ]
    out_specs = [
        pl.BlockSpec((tm, ATT_Q_W), tile),
        pl.BlockSpec((None, ATT_KV_W, tm), lambda i: (i // tiles_per_batch, 0, i % tiles_per_batch)),
        pl.BlockSpec((None, ATT_KV_HEADS, tm, LANES), lambda i: (i // tiles_per_batch, 0, i % tiles_per_batch, 0)),
        pl.BlockSpec((tm, GLA_QK_W), tile),
        pl.BlockSpec((tm, GLA_QK_W), tile),
        pl.BlockSpec((tm, GLA_V_W), tile),
        pl.BlockSpec((tm, GLA_QK_W), tile),
        pl.BlockSpec((tm, GLA_QK_W), tile),
        pl.BlockSpec((tm, GLA_V_W), tile),
    ]
    return pl.pallas_call(
        _inproj_kernel,
        grid=(n_tiles,),
        in_specs=in_specs,
        out_specs=out_specs,
        out_shape=out_shape,
        compiler_params=pltpu.CompilerParams(dimension_semantics=("parallel",), vmem_limit_bytes=VMEM_LIMIT),
        name="inproj",
    )(x2d, mod3, mod3, g1, w_in_p, bd, gqk, cos_t, sin_t, w2, b2)


GLA_ROWS = 256
GLA_NCH = GLA_ROWS // GLA_CHUNK


def _gla_kernel(qf_ref, kf_ref, vf_ref, gf_ref, qb_ref, kb_ref, vb_ref, gb_ref, s0_ref, trif_ref, trib_ref,
                of_ref, ob_ref, sfin_ref, state_ref):
    @pl.when(pl.program_id(0) == 0)
    def _():
        state_ref[...] = s0_ref[...]

    stages = []
    for b in range(qf_ref.shape[0]):
        stages.append(_gla_direction(qf_ref.at[b], kf_ref.at[b], vf_ref.at[b], gf_ref.at[b], trif_ref,
                                     of_ref.at[b], state_ref.at[b, 0], reverse=False))
        stages.append(_gla_direction(qb_ref.at[b], kb_ref.at[b], vb_ref.at[b], gb_ref.at[b], trib_ref,
                                     ob_ref.at[b], state_ref.at[b, 1], reverse=True))
    for _ in zip(*stages):
        pass
    sfin_ref[...] = state_ref[...]


def _gla_direction(q_ref, k_ref, v_ref, g_ref, tri_ref, o_ref, state_ref, *, reverse):
    r = GLA_ROWS
    g = g_ref[...]
    g_hi, g_lo = _split_bf16(g)
    tri = tri_ref[...]
    b = _dot(tri, g_hi) + _dot(tri, g_lo)
    yield
    ends = [c * GLA_CHUNK + (0 if reverse else GLA_CHUNK - 1) for c in range(GLA_NCH)]
    tot = [b[e:e + 1, :] for e in ends]
    b_tot = jnp.concatenate([jnp.broadcast_to(t, (GLA_CHUNK, GLA_QK_W)) for t in tot], axis=0)
    dec_t = jnp.concatenate(
        [jnp.concatenate([jnp.broadcast_to(tot[c], (GLA_VAL_DIM, GLA_QK_W)),
                          jnp.broadcast_to(tot[c + 1], (GLA_VAL_DIM, GLA_QK_W))], axis=0).T
         for c in range(0, GLA_NCH, 2)], axis=1)
    dec_t = jnp.exp(dec_t)

    q_in = q_ref[...] * jnp.exp(b)
    k = k_ref[...]
    k_in = (k * jnp.exp(-b)).astype(BF16)
    k_up_t = (k * jnp.exp(b_tot - b)).T
    v = v_ref[...].astype(BF16)

    row = lax.broadcasted_iota(jnp.int32, (r, r), 0)
    col = lax.broadcasted_iota(jnp.int32, (r, r), 1)
    if reverse:
        dist, reach = col - row, (GLA_CHUNK - 1) - row % GLA_CHUNK
    else:
        dist, reach = row - col, row % GLA_CHUNK
    att_mask = dist.astype(jnp.uint32) <= reach.astype(jnp.uint32)
    lane_head = lax.broadcasted_iota(jnp.int32, (r, GLA_QK_W), 1) // GLA_KEY_DIM
    col_chunk = lax.broadcasted_iota(jnp.int32, (GLA_KEY_DIM, r), 1) // GLA_CHUNK

    q_heads = []
    o_intra = []
    upd = []
    for h in range(GLA_HEADS):
        qh = jnp.where(lane_head == h, q_in, 0.0).astype(BF16)
        q_heads.append(qh)
        att = lax.dot_general(qh, k_in, (((1,), (1,)), ((), ())), preferred_element_type=F32)
        yield
        att = jnp.where(att_mask, att, 0.0).astype(BF16)
        vh = v[:, h * GLA_VAL_DIM:(h + 1) * GLA_VAL_DIM]
        o_intra.append(_dot(att, vh))
        kt_h = k_up_t[h * GLA_KEY_DIM:(h + 1) * GLA_KEY_DIM, :]
        lhs = jnp.concatenate([jnp.where(col_chunk == c, kt_h, 0.0) for c in range(GLA_NCH)], axis=0)
        upd.append(_dot(lhs.astype(BF16), vh))
        yield

    state = state_ref[...]
    order = range(GLA_NCH - 1, -1, -1) if reverse else range(GLA_NCH)
    for c in order:
        rows = slice(c * GLA_CHUNK, (c + 1) * GLA_CHUNK)
        q_stack = jnp.concatenate([qh[rows, :] for qh in q_heads], axis=0)
        inter = _dot(q_stack, state.astype(BF16))
        for h in range(GLA_HEADS):
            o_ref[rows, h * GLA_VAL_DIM:(h + 1) * GLA_VAL_DIM] = (
                o_intra[h][rows, :] + inter[h * GLA_CHUNK:(h + 1) * GLA_CHUNK, :])
        u_c = jnp.concatenate([u[rows, :] for u in upd], axis=0)
        state = dec_t[:, c * GLA_VAL_DIM:(c + 1) * GLA_VAL_DIM] * state + u_c
        yield
    state_ref[...] = state
    yield


def _gla(gq, gk, gv, lf, lb, s0, *, batch):
    t = gq.shape[0]
    s = t // batch
    ng = s // GLA_ROWS
    fwd = lambda g: (0, g, 0)
    bwd = lambda g: (0, ng - 1 - g, 0)
    const = lambda g: (0, 0)
    state_w = GLA_HEADS * GLA_KEY_DIM
    qk = lambda m: pl.BlockSpec((batch, GLA_ROWS, GLA_QK_W), m)
    val = lambda m: pl.BlockSpec((batch, GLA_ROWS, GLA_V_W), m)
    state_spec = pl.BlockSpec((batch, 2, state_w, GLA_VAL_DIM), lambda g: (0, 0, 0, 0))
    per_sample = lambda a: a.reshape(batch, s, a.shape[-1])
    gq, gk, gv, lf, lb = (per_sample(a) for a in (gq, gk, gv, lf, lb))
    o_f, o_b, s_fin = pl.pallas_call(
        _gla_kernel,
        grid=(ng,),
        in_specs=[qk(fwd), qk(fwd), val(fwd), qk(fwd), qk(bwd), qk(bwd), val(bwd), qk(bwd), state_spec,
                  pl.BlockSpec((GLA_ROWS, GLA_ROWS), const), pl.BlockSpec((GLA_ROWS, GLA_ROWS), const)],
        out_specs=[val(fwd), val(bwd), state_spec],
        out_shape=[jax.ShapeDtypeStruct((batch, s, GLA_V_W), F32), jax.ShapeDtypeStruct((batch, s, GLA_V_W), F32),
                   jax.ShapeDtypeStruct((batch, 2, state_w, GLA_VAL_DIM), F32)],
        scratch_shapes=[pltpu.VMEM((batch, 2, state_w, GLA_VAL_DIM), F32)],
        compiler_params=pltpu.CompilerParams(dimension_semantics=("arbitrary",), vmem_limit_bytes=VMEM_LIMIT),
        name="gla",
    )(gq, gk, gv, lf, gq, gk, gv, lb, s0, _gla_tri(False), _gla_tri(True))
    return o_f.reshape(t, GLA_V_W), o_b.reshape(t, GLA_V_W), s_fin


def _gla_tri(reverse):
    r = np.arange(GLA_ROWS)
    same = (r[:, None] // GLA_CHUNK) == (r[None, :] // GLA_CHUNK)
    tri = same & ((r[None, :] >= r[:, None]) if reverse else (r[None, :] <= r[:, None]))
    return jnp.asarray(tri.astype(np.float32), dtype=BF16)


ATT_TQ = 1024
ATT_TK = 768
NEG_BIG = -1e30
MAX_UNSHIFTED_SCORE = 64.0


def _attn_kernel(q_ref, kt_ref, v_ref, o_ref, qs_ref, m_ref, acc_ref, *, tq, tk, nk, online):
    q = q_ref[...]
    for h in range(ATT_GROUP):
        qs_ref[h * tq:(h + 1) * tq, :] = q[:, h * ATT_HEAD_DIM:(h + 1) * ATT_HEAD_DIM]
    acc_ref[...] = jnp.zeros(acc_ref.shape, F32)
    if online:
        m_ref[...] = jnp.full(m_ref.shape, NEG_BIG, F32)

    def body(j, carry):
        ks = pl.multiple_of(j * tk, tk)
        s = _dot(qs_ref[...], kt_ref[:, pl.ds(ks, tk)])
        if online:
            m_prev = m_ref[...]
            m_new = jnp.maximum(m_prev, jnp.max(s, axis=-1, keepdims=True))
            p = jnp.exp2(s - m_new).astype(BF16)
            acc_ref[...] = jnp.exp2(m_prev - m_new) * acc_ref[...] + _dot(p, v_ref[pl.ds(ks, tk), :])
            m_ref[...] = m_new
        else:
            acc_ref[...] += _dot(jnp.exp2(s).astype(BF16), v_ref[pl.ds(ks, tk), :])
        return carry

    lax.fori_loop(0, nk, body, 0)
    acc = acc_ref[...]
    out = acc[:, :ATT_HEAD_DIM] / acc[:, ATT_HEAD_DIM:ATT_HEAD_DIM + 1]
    for h in range(ATT_GROUP):
        o_ref[:, h * ATT_HEAD_DIM:(h + 1) * ATT_HEAD_DIM] = out[h * tq:(h + 1) * tq, :]


def _attention_call(q, kt, vext, *, batch, seq, online):
    tq, tk = ATT_TQ, ATT_TK
    klen = kt.shape[-1]
    nk = klen // tk
    nq = seq // tq
    gw = ATT_GROUP * ATT_HEAD_DIM
    return pl.pallas_call(
        functools.partial(_attn_kernel, tq=tq, tk=tk, nk=nk, online=online),
        grid=(batch, ATT_KV_HEADS, nq),
        in_specs=[pl.BlockSpec((tq, gw), lambda b, kh, i: (b * nq + i, kh)),
                  pl.BlockSpec((None, ATT_HEAD_DIM, klen), lambda b, kh, i: (b, kh, 0)),
                  pl.BlockSpec((None, None, klen, LANES), lambda b, kh, i: (b, kh, 0, 0))],
        out_specs=pl.BlockSpec((tq, gw), lambda b, kh, i: (b * nq + i, kh)),
        out_shape=jax.ShapeDtypeStruct((batch * seq, ATT_Q_W), F32),
        scratch_shapes=[pltpu.VMEM((ATT_GROUP * tq, ATT_HEAD_DIM), BF16),
                        pltpu.VMEM((ATT_GROUP * tq, 1), F32),
                        pltpu.VMEM((ATT_GROUP * tq, LANES), F32)],
        compiler_params=pltpu.CompilerParams(dimension_semantics=("parallel", "parallel", "parallel"),
                                             vmem_limit_bytes=VMEM_LIMIT),
        name="attention_online" if online else "attention",
    )(q, kt, vext)


def _attention(q, kt, vext, score_bound, *, batch, seq):
    call = functools.partial(_attention_call, batch=batch, seq=seq)
    return lax.cond(score_bound <= MAX_UNSHIFTED_SCORE,
                    functools.partial(call, online=False), functools.partial(call, online=True), q, kt, vext)


def _merge_kernel(att_ref, of_ref, ob_ref, gr_ref, x_ref, gt_ref, ga_ref, gg_ref, w_ref, sh2_ref, sc2_ref, g2_ref,
                  o_ref, h2_ref):
    a = att_ref[...]
    a = a * lax.rsqrt(jnp.mean(a * a, axis=-1, keepdims=True) + EPS) * ga_ref[...]
    og = of_ref[...] + ob_ref[...]
    gr = gr_ref[...]
    gg = gg_ref[...]
    parts = [a.astype(BF16)]
    for h in range(GLA_HEADS):
        sl = slice(h * GLA_VAL_DIM, (h + 1) * GLA_VAL_DIM)
        oh = og[:, sl]
        oh = oh * lax.rsqrt(jnp.mean(oh * oh, axis=-1, keepdims=True) + EPS) * gg
        grh = gr[:, sl]
        parts.append((oh * (grh * _sigmoid(grh))).astype(BF16))
    mix = jnp.concatenate(parts, axis=-1)
    x1 = x_ref[...] + gt_ref[...] * _dot(mix, w_ref[...])
    o_ref[...] = x1
    xn = x1 * lax.rsqrt(jnp.mean(x1 * x1, axis=-1, keepdims=True) + EPS)
    h2_ref[...] = (xn * (g2_ref[...] * (1.0 + sc2_ref[...])) + sh2_ref[...]).astype(BF16)


def _merge(att_o, o_f, o_b, gr, x2d, mod3, g_att, g_gla, w_out, g2, *, tm, tiles_per_batch):
    t, d = x2d.shape
    tile = lambda i: (i, 0)
    const = lambda i: (0, 0)
    row = lambda i: i // tiles_per_batch
    return pl.pallas_call(
        _merge_kernel,
        grid=(t // tm,),
        in_specs=[pl.BlockSpec((tm, ATT_Q_W), tile),
                  pl.BlockSpec((tm, GLA_V_W), tile),
                  pl.BlockSpec((tm, GLA_V_W), tile),
                  pl.BlockSpec((tm, GLA_V_W), tile),
                  pl.BlockSpec((tm, d), tile),
                  pl.BlockSpec((None, 1, d), lambda i: (i // tiles_per_batch, 0, 2)),
                  pl.BlockSpec((1, ATT_Q_W), const),
                  pl.BlockSpec((1, GLA_VAL_DIM), const),
                  pl.BlockSpec((ATT_Q_W + GLA_V_W, d), const),
                  pl.BlockSpec((None, 1, d), lambda i: (row(i), 0, 3)),
                  pl.BlockSpec((None, 1, d), lambda i: (row(i), 0, 4)),
                  pl.BlockSpec((1, d), const)],
        out_specs=[pl.BlockSpec((tm, d), tile), pl.BlockSpec((tm, d), tile)],
        out_shape=[jax.ShapeDtypeStruct((t, d), F32), jax.ShapeDtypeStruct((t, d), BF16)],
        compiler_params=pltpu.CompilerParams(dimension_semantics=("parallel",), vmem_limit_bytes=VMEM_LIMIT),
        name="merge",
    )(att_o, o_f, o_b, gr, x2d, mod3, g_att, g_gla, w_out, mod3, mod3, g2)


HALO = 16
FFN_BLOCK = 256


def _ffn_kernel(h_ref, hp_ref, hn_ref, x_ref, gt_ref, gf_ref, wup_ref, cw_ref, cb_ref, wdn_ref,
                o_ref, hb_ref, acc_ref, u_ref, *, tm, tiles_per_batch, d_ff):
    pos = pl.program_id(0) % tiles_per_batch
    hp = hp_ref[...]
    hn = hn_ref[...]
    hb_ref[0:HALO, :] = jnp.where(pos > 0, hp, jnp.zeros_like(hp))
    hb_ref[HALO:HALO + tm, :] = h_ref[...]
    hb_ref[HALO + tm:, :] = jnp.where(pos < tiles_per_batch - 1, hn, jnp.zeros_like(hn))

    nb = FFN_BLOCK
    n_blocks = d_ff // nb
    gt = gt_ref[...]

    def cols(j, half):
        return pl.ds(pl.multiple_of(half * d_ff + j * nb, LANES), nb)

    def up_project(j, slot):
        for half in range(2):
            u_ref[2 * slot + half] = _dot(hb_ref[...], wup_ref[:, cols(j, half)])

    def conv(j, slot, half):
        u = u_ref.at[2 * slot + half]
        cw = cw_ref[:, cols(j, half)]
        return (u[HALO - 1:HALO - 1 + tm, :] * cw[0:1, :] + u[HALO:HALO + tm, :] * cw[1:2, :]
                + u[HALO + 1:HALO + 1 + tm, :] * cw[2:3, :] + cb_ref[:, cols(j, half)])

    def down_project(j, slot):
        gate = conv(j, slot, 1)
        act = (gate * _sigmoid(gate) * conv(j, slot, 0)).astype(BF16)
        acc_ref[...] += _dot(act, wdn_ref[pl.ds(pl.multiple_of(j * nb, nb), nb), :])

    acc_ref[...] = jnp.zeros(acc_ref.shape, F32)
    up_project(0, 0)

    def body(i, carry):
        j = 2 * i
        up_project(j + 1, 1)
        down_project(j, 0)
        up_project(j + 2, 0)
        down_project(j + 1, 1)
        return carry

    assert n_blocks % 2 == 1
    lax.fori_loop(0, n_blocks // 2, body, 0)
    down_project(n_blocks - 1, 0)

    y = x_ref[...] + gt * acc_ref[...]
    o_ref[...] = y * lax.rsqrt(jnp.mean(y * y, axis=-1, keepdims=True) + EPS) * gf_ref[...]


def _ffn(h2, x1, mod3, g_final, w_up, conv_w, conv_b, w_down, *, tm, tiles_per_batch):
    t, d = x1.shape
    d_ff = w_down.shape[0]
    hb = tm // HALO
    n_halo = t // HALO
    tile = lambda i: (i, 0)
    const = lambda i: (0, 0)
    single = pl.Buffered(1)
    return pl.pallas_call(
        functools.partial(_ffn_kernel, tm=tm, tiles_per_batch=tiles_per_batch, d_ff=d_ff),
        grid=(t // tm,),
        in_specs=[pl.BlockSpec((tm, d), tile),
                  pl.BlockSpec((HALO, d), lambda i: (jnp.maximum(i * hb - 1, 0), 0)),
                  pl.BlockSpec((HALO, d), lambda i: (jnp.minimum((i + 1) * hb, n_halo - 1), 0)),
                  pl.BlockSpec((tm, d), tile),
                  pl.BlockSpec((None, 1, d), lambda i: (i // tiles_per_batch, 0, 5)),
                  pl.BlockSpec((1, d), const),
                  pl.BlockSpec((d, 2 * d_ff), const, pipeline_mode=single),
                  pl.BlockSpec((8, 2 * d_ff), const),
                  pl.BlockSpec((1, 2 * d_ff), const),
                  pl.BlockSpec((d_ff, d), const, pipeline_mode=single)],
        out_specs=pl.BlockSpec((tm, d), tile),
        out_shape=jax.ShapeDtypeStruct((t, d), F32),
        scratch_shapes=[pltpu.VMEM((tm + 2 * HALO, d), BF16),
                        pltpu.VMEM((tm, d), F32),
                        pltpu.VMEM((4, tm + 2 * HALO, FFN_BLOCK), F32)],
        compiler_params=pltpu.CompilerParams(dimension_semantics=("parallel",), vmem_limit_bytes=VMEM_LIMIT),
        name="ffn",
    )(h2, h2, h2, x1, mod3, g_final, w_up, conv_w, conv_b, w_down)


def _head_perm(n_heads):
    j = np.arange(ATT_HEAD_DIM // 2)
    one = np.concatenate([2 * j, 2 * j + 1])
    return np.concatenate([h * ATT_HEAD_DIM + one for h in range(n_heads)])


def _rope_tables(seq):
    rows = seq // GRID_W
    row = jnp.broadcast_to(jnp.arange(rows, dtype=F32)[:, None], (rows, GRID_W)).reshape(-1)
    col = jnp.broadcast_to(jnp.arange(GRID_W, dtype=F32)[None, :], (rows, GRID_W)).reshape(-1)
    half = ATT_HEAD_DIM // 2
    freqs = ROPE_THETA ** (-jnp.arange(0, half, 2, dtype=F32) / half)
    ang = jnp.concatenate([row[:, None] * freqs, col[:, None] * freqs], axis=-1)
    cos, sin = jnp.cos(ang), jnp.sin(ang)
    cos_t = jnp.tile(jnp.concatenate([cos, cos], axis=-1), (1, LANES // ATT_HEAD_DIM))
    sin_t = jnp.tile(jnp.concatenate([-sin, sin], axis=-1), (1, LANES // ATT_HEAD_DIM))
    return cos_t, sin_t


def kernel(x, c, ctx, c_ctx, w_mod, b_mod, g_norm1, w_in, g_q, g_k, w_gate_fwd, b_gate_fwd, w_gate_bwd, b_gate_bwd,
           g_att_out, g_gla_out, w_out, g_norm2, w_up, conv_w, conv_b, w_down, g_final):
    batch, seq, d = x.shape
    ctx_len = ctx.shape[1]
    depth = w_mod.shape[0]
    assert depth == 1 and batch + 1 <= 8
    assert seq % 512 == 0 and ctx_len % GLA_ROWS == 0 and (seq + ctx_len) % ATT_TK == 0

    cond = jnp.zeros((8, d), F32).at[:batch].set(c).at[batch].set(c_ctx)
    mod = _modulation(cond, w_mod[0], b_mod[0])
    mod3 = mod.reshape(8, 1, 6 * d)

    wi = w_in[0]
    o = (ATT_Q_W, ATT_Q_W + ATT_KV_W, ATT_Q_W + 2 * ATT_KV_W)
    o_gq = o[2]
    o_gk = o_gq + GLA_QK_W
    o_gv = o_gk + GLA_QK_W
    o_gl = o_gv + GLA_V_W
    o_gr = o_gl + 2 * GLA_GATE_RANK
    w_in_p = jnp.concatenate([
        wi[:, :o[0]][:, _head_perm(ATT_HEADS)],
        wi[:, o[0]:o[1]][:, _head_perm(ATT_KV_HEADS)],
        wi[:, o[1]:o_gv],
        wi[:, o_gv:o_gl],
        wi[:, o_gr:],
        wi[:, o_gl:o_gr],
        jnp.zeros((d, GATE_PAD_W - 2 * GLA_GATE_RANK), F32)], axis=1).astype(BF16)
    perm1 = _head_perm(1)
    q_scale = (ATT_HEAD_DIM ** -0.5) * math.log2(math.e)
    gqk = jnp.concatenate([jnp.tile(g_q[0][perm1] * q_scale, ATT_HEADS),
                           jnp.tile(g_k[0][perm1], ATT_KV_HEADS)]).reshape(1, QK_W)
    blk = np.arange(256) // ATT_HEAD_DIM
    bd = jnp.asarray((blk[:, None] == blk[None, :]).astype(np.float32), dtype=BF16)
    w2 = jnp.zeros((GATE_PAD_W, 2 * GLA_QK_W), F32)
    w2 = w2.at[:GLA_GATE_RANK, :GLA_QK_W].set(w_gate_fwd[0])
    w2 = w2.at[GLA_GATE_RANK:2 * GLA_GATE_RANK, GLA_QK_W:].set(w_gate_bwd[0]).astype(BF16)
    b2 = jnp.concatenate([b_gate_fwd[0], b_gate_bwd[0]]).reshape(1, 2 * GLA_QK_W)
    g1 = g_norm1[0].reshape(1, d)

    cos_t, sin_t = _rope_tables(seq)
    ones_t = jnp.ones((ctx_len, LANES), F32)
    zeros_t = jnp.zeros((ctx_len, LANES), F32)

    tm = 512
    tpb = seq // tm
    x2d = x.reshape(batch * seq, d)
    ctx2d = ctx.reshape(batch * ctx_len, d)

    shared = (g1, w_in_p, bd, gqk)
    _, ckt, cv, cgq, cgk, cgv, clf, clb, _ = _inproj(
        ctx2d, mod3, lambda b: batch, *shared, ones_t, zeros_t, w2, b2,
        tm=ctx_len, tiles_per_batch=1, batch=batch)
    q, kt, vext, gq, gk, gv, lf, lb, gr = _inproj(
        x2d, mod3, lambda b: b, *shared, cos_t, sin_t, w2, b2, tm=tm, tiles_per_batch=tpb, batch=batch)

    zero_state = jnp.zeros((batch, 2, GLA_HEADS * GLA_KEY_DIM, GLA_VAL_DIM), F32)
    _, _, s_ctx = _gla(cgq, cgk, cgv, clf, clb, zero_state, batch=batch)
    o_f, o_b, _ = _gla(gq, gk, gv, lf, lb, s_ctx, batch=batch)

    kt_all = jnp.concatenate([ckt, kt], axis=-1)
    v_all = jnp.concatenate([cv, vext], axis=2)
    score_bound = ATT_HEAD_DIM * jnp.max(jnp.abs(gqk[0, :ATT_Q_W])) * jnp.max(jnp.abs(gqk[0, ATT_Q_W:]))
    att_o = _attention(q, kt_all, v_all, score_bound, batch=batch, seq=seq)

    x1, h2 = _merge(att_o, o_f, o_b, gr, x2d, mod3, g_att_out[0].reshape(1, ATT_Q_W),
                    g_gla_out[0].reshape(1, GLA_VAL_DIM), w_out[0].astype(BF16), g_norm2[0].reshape(1, d),
                    tm=tm, tiles_per_batch=tpb)

    cw8 = jnp.zeros((8, conv_w.shape[-1]), F32).at[:CONV_WIDTH].set(conv_w[0])
    out = _ffn(h2, x1, mod3, g_final.reshape(1, d), w_up[0].astype(BF16), cw8,
               conv_b[0].reshape(1, -1), w_down[0].astype(BF16), tm=tm, tiles_per_batch=tpb)
    return out.reshape(batch, seq, d)
```

```python
import functools
import math

import numpy as np
import jax
import jax.numpy as jnp
from jax import lax
from jax.experimental import pallas as pl
from jax.experimental.pallas import tpu as pltpu

GRID_W = 64
ATT_HEADS = 8
ATT_KV_HEADS = 2
ATT_HEAD_DIM = 64
ATT_GROUP = ATT_HEADS // ATT_KV_HEADS
GLA_HEADS = 4
GLA_KEY_DIM = 64
GLA_VAL_DIM = 128
GLA_GATE_RANK = 16
GLA_GATE_NORMALIZER = 16.0
GLA_CHUNK = 64
CONV_WIDTH = 3
ROPE_THETA = 10000.0
EPS = 1e-6

ATT_Q_W = ATT_HEADS * ATT_HEAD_DIM
ATT_KV_W = ATT_KV_HEADS * ATT_HEAD_DIM
GLA_QK_W = GLA_HEADS * GLA_KEY_DIM
GLA_V_W = GLA_HEADS * GLA_VAL_DIM
LANES = 128
GATE_PAD_W = LANES
VMEM_LIMIT = 56 * 1024 * 1024

F32 = jnp.float32
BF16 = jnp.bfloat16


def _dot(a, b):
    return jnp.dot(a, b, preferred_element_type=F32)


def _split_bf16(a):
    hi = a.astype(BF16)
    lo = (a - hi.astype(F32)).astype(BF16)
    return hi, lo


def _sigmoid(z):
    return 1.0 / (1.0 + jnp.exp(-z))


def _mod_kernel(c_ref, w_ref, b_ref, o_ref):
    c = c_ref[...]
    a_hi, a_lo = _split_bf16(c * _sigmoid(c))
    w_hi, w_lo = _split_bf16(w_ref[...])
    o_ref[...] = _dot(a_hi, w_hi) + _dot(a_lo, w_hi) + _dot(a_hi, w_lo) + b_ref[...]


def _modulation(cond, w_mod, b_mod):
    d = cond.shape[1]
    n = w_mod.shape[1]
    bn = d
    return pl.pallas_call(
        _mod_kernel,
        grid=(n // bn,),
        in_specs=[pl.BlockSpec((8, d), lambda j: (0, 0)),
                  pl.BlockSpec((d, bn), lambda j: (0, j)),
                  pl.BlockSpec((1, bn), lambda j: (0, j))],
        out_specs=pl.BlockSpec((8, bn), lambda j: (0, j)),
        out_shape=jax.ShapeDtypeStruct((8, n), F32),
        compiler_params=pltpu.CompilerParams(dimension_semantics=("parallel",), vmem_limit_bytes=VMEM_LIMIT),
        name="modulation",
    )(cond, w_mod, b_mod.reshape(1, n))


C_AQ, C_AK, C_AV = 0, ATT_Q_W, ATT_Q_W + ATT_KV_W
C_GQ = C_AV + ATT_KV_W
C_GK = C_GQ + GLA_QK_W
C_GV = C_GK + GLA_QK_W
C_GR = C_GV + GLA_V_W
C_GL = C_GR + GLA_V_W
IN_W_PAD = C_GL + GATE_PAD_W
QK_W = ATT_Q_W + ATT_KV_W


def _inproj_kernel(x_ref, *refs):
    tm = x_ref.shape[0]
    tail = tm // 4 if (tm // 4) % LANES == 0 else tm // 2
    first = _inproj_rows(pl.ds(0, tm - tail), x_ref, *refs)
    second = _inproj_rows(pl.ds(tm - tail, tail), x_ref, *refs)
    next(first)
    next(first)
    for _ in second:
        next(first, None)


def _inproj_rows(rows, x_ref, sh_ref, sc_ref, g1_ref, w_ref, bd_ref, gqk_ref, cos_ref, sin_ref, w2_ref, b2_ref,
                 q_ref, kt_ref, v_ref, gq_ref, gk_ref, gv_ref, lf_ref, lb_ref, gr_ref):
    x = x_ref[rows, :]
    n = x.shape[0]
    xn = x * lax.rsqrt(jnp.mean(x * x, axis=-1, keepdims=True) + EPS)
    h = xn * (g1_ref[...] * (1.0 + sc_ref[...])) + sh_ref[...]
    z = _dot(h.astype(BF16), w_ref[...])
    yield

    bd = bd_ref[...]
    sq = (z[:, :QK_W] * z[:, :QK_W]).astype(BF16)
    ss = jnp.concatenate(
        [_dot(sq[:, 0:256], bd), _dot(sq[:, 256:512], bd), _dot(sq[:, 512:640], bd[:128, :128])], axis=-1)
    gl_hi, gl_lo = _split_bf16(z[:, C_GL:C_GL + GATE_PAD_W])
    w2 = w2_ref[...]
    zg = _dot(gl_hi, w2) + _dot(gl_lo, w2) + b2_ref[...]
    yield

    y = z[:, :QK_W] * lax.rsqrt(ss * (1.0 / ATT_HEAD_DIM) + EPS) * gqk_ref[...]
    lane = lax.broadcasted_iota(jnp.int32, (n, LANES), 1)
    first_half = (lane % ATT_HEAD_DIM) < (ATT_HEAD_DIM // 2)
    cos = cos_ref[rows, :]
    sin = sin_ref[rows, :]
    roped = []
    for j in range(QK_W // LANES):
        yc = y[:, j * LANES:(j + 1) * LANES]
        partner = jnp.where(first_half, pltpu.roll(yc, LANES - 32, 1), pltpu.roll(yc, 32, 1))
        roped.append(yc * cos + partner * sin)
    q_ref[rows, :] = jnp.concatenate(roped[:4], axis=-1).astype(BF16)
    kt_ref[:, rows] = roped[4].T.astype(BF16)
    yield

    v = z[:, C_AV:C_AV + ATT_KV_W]
    tail = jnp.where(lane == ATT_HEAD_DIM, 1.0, 0.0)
    v_ref[0, rows, :] = jnp.where(lane < ATT_HEAD_DIM, v, tail).astype(BF16)
    v_ref[1, rows, :] = jnp.where(lane < ATT_HEAD_DIM, pltpu.roll(v, ATT_HEAD_DIM, 1), tail).astype(BF16)

    gq_ref[rows, :] = z[:, C_GQ:C_GQ + GLA_QK_W] * (GLA_KEY_DIM ** -0.5)
    gk_ref[rows, :] = z[:, C_GK:C_GK + GLA_QK_W]
    gv_ref[rows, :] = z[:, C_GV:C_GV + GLA_V_W].astype(gv_ref.dtype)
    gr_ref[rows, :] = z[:, C_GR:C_GR + GLA_V_W].astype(gr_ref.dtype)

    lg = -(jnp.maximum(-zg, 0.0) + jnp.log(1.0 + jnp.exp(-jnp.abs(zg)))) * (1.0 / GLA_GATE_NORMALIZER)
    lf_ref[rows, :] = lg[:, :GLA_QK_W]
    lb_ref[rows, :] = lg[:, GLA_QK_W:]
    yield


def _inproj(x2d, mod3, row_of_tile, g1, w_in_p, bd, gqk, cos_t, sin_t, w2, b2, *, tm, tiles_per_batch, batch):
    t, d = x2d.shape
    s = tm * tiles_per_batch
    n_tiles = t // tm
    ntab = cos_t.shape[0] // tm

    def row(i):
        return row_of_tile(i // tiles_per_batch)

    const = lambda i: (0, 0)
    tile = lambda i: (i, 0)
    in_specs = [
        pl.BlockSpec((tm, d), tile),
        pl.BlockSpec((None, 1, d), lambda i: (row(i), 0, 0)),
        pl.BlockSpec((None, 1, d), lambda i: (row(i), 0, 1)),
        pl.BlockSpec((1, d), const),
        pl.BlockSpec((d, IN_W_PAD), const),
        pl.BlockSpec((256, 256), const),
        pl.BlockSpec((1, QK_W), const),
        pl.BlockSpec((tm, LANES), lambda i: (i % ntab, 0)),
        pl.BlockSpec((tm, LANES), lambda i: (i % ntab, 0)),
        pl.BlockSpec((GATE_PAD_W, 2 * GLA_QK_W), const),
        pl.BlockSpec((1, 2 * GLA_QK_W), const),
    ]
    out_shape = [
        jax.ShapeDtypeStruct((t, ATT_Q_W), BF16),
        jax.ShapeDtypeStruct((batch, ATT_KV_W, s), BF16),
        jax.ShapeDtypeStruct((batch, ATT_KV_HEADS, s, LANES), BF16),
        jax.ShapeDtypeStruct((t, GLA_QK_W), F32),
        jax.ShapeDtypeStruct((t, GLA_QK_W), F32),
        jax.ShapeDtypeStruct((t, GLA_V_W), BF16),
        jax.ShapeDtypeStruct((t, GLA_QK_W), F32),
        jax.ShapeDtypeStruct((t, GLA_QK_W), F32),
        jax.ShapeDtypeStruct((t, GLA_V_W), BF16),
    ]
    out_specs = [
        pl.BlockSpec((tm, ATT_Q_W), tile),
        pl.BlockSpec((None, ATT_KV_W, tm), lambda i: (i // tiles_per_batch, 0, i % tiles_per_batch)),
        pl.BlockSpec((None, ATT_KV_HEADS, tm, LANES), lambda i: (i // tiles_per_batch, 0, i % tiles_per_batch, 0)),
        pl.BlockSpec((tm, GLA_QK_W), tile),
        pl.BlockSpec((tm, GLA_QK_W), tile),
        pl.BlockSpec((tm, GLA_V_W), tile),
        pl.BlockSpec((tm, GLA_QK_W), tile),
        pl.BlockSpec((tm, GLA_QK_W), tile),
        pl.BlockSpec((tm, GLA_V_W), tile),
    ]
    return pl.pallas_call(
        _inproj_kernel,
        grid=(n_tiles,),
        in_specs=in_specs,
        out_specs=out_specs,
        out_shape=out_shape,
        compiler_params=pltpu.CompilerParams(dimension_semantics=("parallel",), vmem_limit_bytes=VMEM_LIMIT),
        name="inproj",
    )(x2d, mod3, mod3, g1, w_in_p, bd, gqk, cos_t, sin_t, w2, b2)


GLA_ROWS = 256
GLA_NCH = GLA_ROWS // GLA_CHUNK


def _gla_kernel(qf_ref, kf_ref, vf_ref, gf_ref, qb_ref, kb_ref, vb_ref, gb_ref, s0_ref, trif_ref, trib_ref,
                of_ref, ob_ref, sfin_ref, state_ref):
    @pl.when(pl.program_id(0) == 0)
    def _():
        state_ref[...] = s0_ref[...]

    stages = []
    for b in range(qf_ref.shape[0]):
        stages.append(_gla_direction(qf_ref.at[b], kf_ref.at[b], vf_ref.at[b], gf_ref.at[b], trif_ref,
                                     of_ref.at[b], state_ref.at[b, 0], reverse=False))
        stages.append(_gla_direction(qb_ref.at[b], kb_ref.at[b], vb_ref.at[b], gb_ref.at[b], trib_ref,
                                     ob_ref.at[b], state_ref.at[b, 1], reverse=True))
    for _ in zip(*stages):
        pass
    sfin_ref[...] = state_ref[...]


def _gla_direction(q_ref, k_ref, v_ref, g_ref, tri_ref, o_ref, state_ref, *, reverse):
    r = GLA_ROWS
    g = g_ref[...]
    g_hi, g_lo = _split_bf16(g)
    tri = tri_ref[...]
    b = _dot(tri, g_hi) + _dot(tri, g_lo)
    yield
    ends = [c * GLA_CHUNK + (0 if reverse else GLA_CHUNK - 1) for c in range(GLA_NCH)]
    tot = [b[e:e + 1, :] for e in ends]
    b_tot = jnp.concatenate([jnp.broadcast_to(t, (GLA_CHUNK, GLA_QK_W)) for t in tot], axis=0)
    dec_t = jnp.concatenate(
        [jnp.concatenate([jnp.broadcast_to(tot[c], (GLA_VAL_DIM, GLA_QK_W)),
                          jnp.broadcast_to(tot[c + 1], (GLA_VAL_DIM, GLA_QK_W))], axis=0).T
         for c in range(0, GLA_NCH, 2)], axis=1)
    dec_t = jnp.exp(dec_t)

    q_in = q_ref[...] * jnp.exp(b)
    k = k_ref[...]
    k_in = (k * jnp.exp(-b)).astype(BF16)
    k_up_t = (k * jnp.exp(b_tot - b)).T
    v = v_ref[...].astype(BF16)

    row = lax.broadcasted_iota(jnp.int32, (r, r), 0)
    col = lax.broadcasted_iota(jnp.int32, (r, r), 1)
    if reverse:
        dist, reach = col - row, (GLA_CHUNK - 1) - row % GLA_CHUNK
    else:
        dist, reach = row - col, row % GLA_CHUNK
    att_mask = dist.astype(jnp.uint32) <= reach.astype(jnp.uint32)
    lane_head = lax.broadcasted_iota(jnp.int32, (r, GLA_QK_W), 1) // GLA_KEY_DIM
    col_chunk = lax.broadcasted_iota(jnp.int32, (GLA_KEY_DIM, r), 1) // GLA_CHUNK

    q_heads = []
    o_intra = []
    upd = []
    for h in range(GLA_HEADS):
        qh = jnp.where(lane_head == h, q_in, 0.0).astype(BF16)
        q_heads.append(qh)
        att = lax.dot_general(qh, k_in, (((1,), (1,)), ((), ())), preferred_element_type=F32)
        yield
        att = jnp.where(att_mask, att, 0.0).astype(BF16)
        vh = v[:, h * GLA_VAL_DIM:(h + 1) * GLA_VAL_DIM]
        o_intra.append(_dot(att, vh))
        kt_h = k_up_t[h * GLA_KEY_DIM:(h + 1) * GLA_KEY_DIM, :]
        lhs = jnp.concatenate([jnp.where(col_chunk == c, kt_h, 0.0) for c in range(GLA_NCH)], axis=0)
        upd.append(_dot(lhs.astype(BF16), vh))
        yield

    state = state_ref[...]
    order = range(GLA_NCH - 1, -1, -1) if reverse else range(GLA_NCH)
    for c in order:
        rows = slice(c * GLA_CHUNK, (c + 1) * GLA_CHUNK)
        q_stack = jnp.concatenate([qh[rows, :] for qh in q_heads], axis=0)
        inter = _dot(q_stack, state.astype(BF16))
        for h in range(GLA_HEADS):
            o_ref[rows, h * GLA_VAL_DIM:(h + 1) * GLA_VAL_DIM] = (
                o_intra[h][rows, :] + inter[h * GLA_CHUNK:(h + 1) * GLA_CHUNK, :]).astype(o_ref.dtype)
        u_c = jnp.concatenate([u[rows, :] for u in upd], axis=0)
        state = dec_t[:, c * GLA_VAL_DIM:(c + 1) * GLA_VAL_DIM] * state + u_c
        yield
    state_ref[...] = state
    yield


def _gla(gq, gk, gv, lf, lb, s0, *, batch):
    t = gq.shape[0]
    s = t // batch
    ng = s // GLA_ROWS
    fwd = lambda g: (0, g, 0)
    bwd = lambda g: (0, ng - 1 - g, 0)
    const = lambda g: (0, 0)
    state_w = GLA_HEADS * GLA_KEY_DIM
    qk = lambda m: pl.BlockSpec((batch, GLA_ROWS, GLA_QK_W), m)
    val = lambda m: pl.BlockSpec((batch, GLA_ROWS, GLA_V_W), m)
    state_spec = pl.BlockSpec((batch, 2, state_w, GLA_VAL_DIM), lambda g: (0, 0, 0, 0))
    per_sample = lambda a: a.reshape(batch, s, a.shape[-1])
    gq, gk, gv, lf, lb = (per_sample(a) for a in (gq, gk, gv, lf, lb))
    o_f, o_b, s_fin = pl.pallas_call(
        _gla_kernel,
        grid=(ng,),
        in_specs=[qk(fwd), qk(fwd), val(fwd), qk(fwd), qk(bwd), qk(bwd), val(bwd), qk(bwd), state_spec,
                  pl.BlockSpec((GLA_ROWS, GLA_ROWS), const), pl.BlockSpec((GLA_ROWS, GLA_ROWS), const)],
        out_specs=[val(fwd), val(bwd), state_spec],
        out_shape=[jax.ShapeDtypeStruct((batch, s, GLA_V_W), BF16), jax.ShapeDtypeStruct((batch, s, GLA_V_W), BF16),
                   jax.ShapeDtypeStruct((batch, 2, state_w, GLA_VAL_DIM), F32)],
        scratch_shapes=[pltpu.VMEM((batch, 2, state_w, GLA_VAL_DIM), F32)],
        compiler_params=pltpu.CompilerParams(dimension_semantics=("arbitrary",), vmem_limit_bytes=VMEM_LIMIT),
        name="gla",
    )(gq, gk, gv, lf, gq, gk, gv, lb, s0, _gla_tri(False), _gla_tri(True))
    return o_f.reshape(t, GLA_V_W), o_b.reshape(t, GLA_V_W), s_fin


def _gla_tri(reverse):
    r = np.arange(GLA_ROWS)
    same = (r[:, None] // GLA_CHUNK) == (r[None, :] // GLA_CHUNK)
    tri = same & ((r[None, :] >= r[:, None]) if reverse else (r[None, :] <= r[:, None]))
    return jnp.asarray(tri.astype(np.float32), dtype=BF16)


ATT_TQ = 1024
ATT_TK = 768
NEG_BIG = -1e30
MAX_UNSHIFTED_SCORE = 64.0


def _attn_kernel(q_ref, kt_ref, v_ref, o_ref, qs_ref, m_ref, acc_ref, *, tq, tk, nk, online):
    q = q_ref[...]
    for h in range(ATT_GROUP):
        qs_ref[h * tq:(h + 1) * tq, :] = q[:, h * ATT_HEAD_DIM:(h + 1) * ATT_HEAD_DIM]
    acc_ref[...] = jnp.zeros(acc_ref.shape, F32)
    if online:
        m_ref[...] = jnp.full(m_ref.shape, NEG_BIG, F32)

    def body(j, carry):
        ks = pl.multiple_of(j * tk, tk)
        s = _dot(qs_ref[...], kt_ref[:, pl.ds(ks, tk)])
        if online:
            m_prev = m_ref[...]
            m_new = jnp.maximum(m_prev, jnp.max(s, axis=-1, keepdims=True))
            p = jnp.exp2(s - m_new).astype(BF16)
            acc_ref[...] = jnp.exp2(m_prev - m_new) * acc_ref[...] + _dot(p, v_ref[pl.ds(ks, tk), :])
            m_ref[...] = m_new
        else:
            acc_ref[...] += _dot(jnp.exp2(s).astype(BF16), v_ref[pl.ds(ks, tk), :])
        return carry

    lax.fori_loop(0, nk, body, 0)
    acc = acc_ref[...]
    out = acc[:, :ATT_HEAD_DIM] / acc[:, ATT_HEAD_DIM:ATT_HEAD_DIM + 1]
    for h in range(ATT_GROUP):
        o_ref[:, h * ATT_HEAD_DIM:(h + 1) * ATT_HEAD_DIM] = out[h * tq:(h + 1) * tq, :].astype(o_ref.dtype)


def _attention_call(q, kt, vext, *, batch, seq, online):
    tq, tk = ATT_TQ, ATT_TK
    klen = kt.shape[-1]
    nk = klen // tk
    nq = seq // tq
    gw = ATT_GROUP * ATT_HEAD_DIM
    return pl.pallas_call(
        functools.partial(_attn_kernel, tq=tq, tk=tk, nk=nk, online=online),
        grid=(batch, ATT_KV_HEADS, nq),
        in_specs=[pl.BlockSpec((tq, gw), lambda b, kh, i: (b * nq + i, kh)),
                  pl.BlockSpec((None, ATT_HEAD_DIM, klen), lambda b, kh, i: (b, kh, 0)),
                  pl.BlockSpec((None, None, klen, LANES), lambda b, kh, i: (b, kh, 0, 0))],
        out_specs=pl.BlockSpec((tq, gw), lambda b, kh, i: (b * nq + i, kh)),
        out_shape=jax.ShapeDtypeStruct((batch * seq, ATT_Q_W), BF16),
        scratch_shapes=[pltpu.VMEM((ATT_GROUP * tq, ATT_HEAD_DIM), BF16),
                        pltpu.VMEM((ATT_GROUP * tq, 1), F32),
                        pltpu.VMEM((ATT_GROUP * tq, LANES), F32)],
        compiler_params=pltpu.CompilerParams(dimension_semantics=("parallel", "parallel", "parallel"),
                                             vmem_limit_bytes=VMEM_LIMIT),
        name="attention_online" if online else "attention",
    )(q, kt, vext)


def _attention(q, kt, vext, score_bound, *, batch, seq):
    call = functools.partial(_attention_call, batch=batch, seq=seq)
    return lax.cond(score_bound <= MAX_UNSHIFTED_SCORE,
                    functools.partial(call, online=False), functools.partial(call, online=True), q, kt, vext)


def _merge_kernel(att_ref, of_ref, ob_ref, gr_ref, x_ref, gt_ref, ga_ref, gg_ref, w_ref, sh2_ref, sc2_ref, g2_ref,
                  o_ref, h2_ref):
    a = att_ref[...].astype(F32)
    a = a * lax.rsqrt(jnp.mean(a * a, axis=-1, keepdims=True) + EPS) * ga_ref[...]
    og = of_ref[...].astype(F32) + ob_ref[...].astype(F32)
    gr = gr_ref[...].astype(F32)
    gg = gg_ref[...]
    parts = [a.astype(BF16)]
    for h in range(GLA_HEADS):
        sl = slice(h * GLA_VAL_DIM, (h + 1) * GLA_VAL_DIM)
        oh = og[:, sl]
        oh = oh * lax.rsqrt(jnp.mean(oh * oh, axis=-1, keepdims=True) + EPS) * gg
        grh = gr[:, sl]
        parts.append((oh * (grh * _sigmoid(grh))).astype(BF16))
    mix = jnp.concatenate(parts, axis=-1)
    x1 = x_ref[...] + gt_ref[...] * _dot(mix, w_ref[...])
    o_ref[...] = x1
    xn = x1 * lax.rsqrt(jnp.mean(x1 * x1, axis=-1, keepdims=True) + EPS)
    h2_ref[...] = (xn * (g2_ref[...] * (1.0 + sc2_ref[...])) + sh2_ref[...]).astype(BF16)


def _merge(att_o, o_f, o_b, gr, x2d, mod3, g_att, g_gla, w_out, g2, *, tm, tiles_per_batch):
    t, d = x2d.shape
    tile = lambda i: (i, 0)
    const = lambda i: (0, 0)
    row = lambda i: i // tiles_per_batch
    return pl.pallas_call(
        _merge_kernel,
        grid=(t // tm,),
        in_specs=[pl.BlockSpec((tm, ATT_Q_W), tile),
                  pl.BlockSpec((tm, GLA_V_W), tile),
                  pl.BlockSpec((tm, GLA_V_W), tile),
                  pl.BlockSpec((tm, GLA_V_W), tile),
                  pl.BlockSpec((tm, d), tile),
                  pl.BlockSpec((None, 1, d), lambda i: (i // tiles_per_batch, 0, 2)),
                  pl.BlockSpec((1, ATT_Q_W), const),
                  pl.BlockSpec((1, GLA_VAL_DIM), const),
                  pl.BlockSpec((ATT_Q_W + GLA_V_W, d), const),
                  pl.BlockSpec((None, 1, d), lambda i: (row(i), 0, 3)),
                  pl.BlockSpec((None, 1, d), lambda i: (row(i), 0, 4)),
                  pl.BlockSpec((1, d), const)],
        out_specs=[pl.BlockSpec((tm, d), tile), pl.BlockSpec((tm, d), tile)],
        out_shape=[jax.ShapeDtypeStruct((t, d), F32), jax.ShapeDtypeStruct((t, d), BF16)],
        compiler_params=pltpu.CompilerParams(dimension_semantics=("parallel",), vmem_limit_bytes=VMEM_LIMIT),
        name="merge",
    )(att_o, o_f, o_b, gr, x2d, mod3, g_att, g_gla, w_out, mod3, mod3, g2)


HALO = 16
FFN_BLOCK = 256


def _ffn_kernel(h_ref, hp_ref, hn_ref, x_ref, gt_ref, gf_ref, wup_ref, cw_ref, cb_ref, wdn_ref,
                o_ref, hb_ref, acc_ref, u_ref, *, tm, tiles_per_batch, d_ff):
    pos = pl.program_id(0) % tiles_per_batch
    hp = hp_ref[...]
    hn = hn_ref[...]
    hb_ref[0:HALO, :] = jnp.where(pos > 0, hp, jnp.zeros_like(hp))
    hb_ref[HALO:HALO + tm, :] = h_ref[...]
    hb_ref[HALO + tm:, :] = jnp.where(pos < tiles_per_batch - 1, hn, jnp.zeros_like(hn))

    nb = FFN_BLOCK
    n_blocks = d_ff // nb
    gt = gt_ref[...]

    def cols(j, half):
        return pl.ds(pl.multiple_of(half * d_ff + j * nb, LANES), nb)

    def up_project(j, slot):
        for half in range(2):
            u_ref[2 * slot + half] = _dot(hb_ref[...], wup_ref[:, cols(j, half)])

    def conv(j, slot, half):
        u = u_ref.at[2 * slot + half]
        cw = cw_ref[:, cols(j, half)]
        return (u[HALO - 1:HALO - 1 + tm, :] * cw[0:1, :] + u[HALO:HALO + tm, :] * cw[1:2, :]
                + u[HALO + 1:HALO + 1 + tm, :] * cw[2:3, :] + cb_ref[:, cols(j, half)])

    def down_project(j, slot):
        gate = conv(j, slot, 1)
        act = (gate * _sigmoid(gate) * conv(j, slot, 0)).astype(BF16)
        acc_ref[...] += _dot(act, wdn_ref[pl.ds(pl.multiple_of(j * nb, nb), nb), :])

    acc_ref[...] = jnp.zeros(acc_ref.shape, F32)
    up_project(0, 0)

    def body(i, carry):
        j = 2 * i
        up_project(j + 1, 1)
        down_project(j, 0)
        up_project(j + 2, 0)
        down_project(j + 1, 1)
        return carry

    assert n_blocks % 2 == 1
    lax.fori_loop(0, n_blocks // 2, body, 0)
    down_project(n_blocks - 1, 0)

    y = x_ref[...] + gt * acc_ref[...]
    o_ref[...] = y * lax.rsqrt(jnp.mean(y * y, axis=-1, keepdims=True) + EPS) * gf_ref[...]


def _ffn(h2, x1, mod3, g_final, w_up, conv_w, conv_b, w_down, *, tm, tiles_per_batch):
    t, d = x1.shape
    d_ff = w_down.shape[0]
    hb = tm // HALO
    n_halo = t // HALO
    tile = lambda i: (i, 0)
    const = lambda i: (0, 0)
    single = pl.Buffered(1)
    return pl.pallas_call(
        functools.partial(_ffn_kernel, tm=tm, tiles_per_batch=tiles_per_batch, d_ff=d_ff),
        grid=(t // tm,),
        in_specs=[pl.BlockSpec((tm, d), tile),
                  pl.BlockSpec((HALO, d), lambda i: (jnp.maximum(i * hb - 1, 0), 0)),
                  pl.BlockSpec((HALO, d), lambda i: (jnp.minimum((i + 1) * hb, n_halo - 1), 0)),
                  pl.BlockSpec((tm, d), tile),
                  pl.BlockSpec((None, 1, d), lambda i: (i // tiles_per_batch, 0, 5)),
                  pl.BlockSpec((1, d), const),
                  pl.BlockSpec((d, 2 * d_ff), const, pipeline_mode=single),
                  pl.BlockSpec((8, 2 * d_ff), const),
                  pl.BlockSpec((1, 2 * d_ff), const),
                  pl.BlockSpec((d_ff, d), const, pipeline_mode=single)],
        out_specs=pl.BlockSpec((tm, d), tile),
        out_shape=jax.ShapeDtypeStruct((t, d), F32),
        scratch_shapes=[pltpu.VMEM((tm + 2 * HALO, d), BF16),
                        pltpu.VMEM((tm, d), F32),
                        pltpu.VMEM((4, tm + 2 * HALO, FFN_BLOCK), F32)],
        compiler_params=pltpu.CompilerParams(dimension_semantics=("parallel",), vmem_limit_bytes=VMEM_LIMIT),
        name="ffn",
    )(h2, h2, h2, x1, mod3, g_final, w_up, conv_w, conv_b, w_down)


def _head_perm(n_heads):
    j = np.arange(ATT_HEAD_DIM // 2)
    one = np.concatenate([2 * j, 2 * j + 1])
    return np.concatenate([h * ATT_HEAD_DIM + one for h in range(n_heads)])


def _rope_tables(seq):
    rows = seq // GRID_W
    row = jnp.broadcast_to(jnp.arange(rows, dtype=F32)[:, None], (rows, GRID_W)).reshape(-1)
    col = jnp.broadcast_to(jnp.arange(GRID_W, dtype=F32)[None, :], (rows, GRID_W)).reshape(-1)
    half = ATT_HEAD_DIM // 2
    freqs = ROPE_THETA ** (-jnp.arange(0, half, 2, dtype=F32) / half)
    ang = jnp.concatenate([row[:, None] * freqs, col[:, None] * freqs], axis=-1)
    cos, sin = jnp.cos(ang), jnp.sin(ang)
    cos_t = jnp.tile(jnp.concatenate([cos, cos], axis=-1), (1, LANES // ATT_HEAD_DIM))
    sin_t = jnp.tile(jnp.concatenate([-sin, sin], axis=-1), (1, LANES // ATT_HEAD_DIM))
    return cos_t, sin_t


def kernel(x, c, ctx, c_ctx, w_mod, b_mod, g_norm1, w_in, g_q, g_k, w_gate_fwd, b_gate_fwd, w_gate_bwd, b_gate_bwd,
           g_att_out, g_gla_out, w_out, g_norm2, w_up, conv_w, conv_b, w_down, g_final):
    batch, seq, d = x.shape
    ctx_len = ctx.shape[1]
    depth = w_mod.shape[0]
    assert depth == 1 and batch + 1 <= 8
    assert seq % 512 == 0 and ctx_len % GLA_ROWS == 0 and (seq + ctx_len) % ATT_TK == 0

    cond = jnp.zeros((8, d), F32).at[:batch].set(c).at[batch].set(c_ctx)
    mod = _modulation(cond, w_mod[0], b_mod[0])
    mod3 = mod.reshape(8, 1, 6 * d)

    wi = w_in[0]
    o = (ATT_Q_W, ATT_Q_W + ATT_KV_W, ATT_Q_W + 2 * ATT_KV_W)
    o_gq = o[2]
    o_gk = o_gq + GLA_QK_W
    o_gv = o_gk + GLA_QK_W
    o_gl = o_gv + GLA_V_W
    o_gr = o_gl + 2 * GLA_GATE_RANK
    w_in_p = jnp.concatenate([
        wi[:, :o[0]][:, _head_perm(ATT_HEADS)],
        wi[:, o[0]:o[1]][:, _head_perm(ATT_KV_HEADS)],
        wi[:, o[1]:o_gv],
        wi[:, o_gv:o_gl],
        wi[:, o_gr:],
        wi[:, o_gl:o_gr],
        jnp.zeros((d, GATE_PAD_W - 2 * GLA_GATE_RANK), F32)], axis=1).astype(BF16)
    perm1 = _head_perm(1)
    q_scale = (ATT_HEAD_DIM ** -0.5) * math.log2(math.e)
    gqk = jnp.concatenate([jnp.tile(g_q[0][perm1] * q_scale, ATT_HEADS),
                           jnp.tile(g_k[0][perm1], ATT_KV_HEADS)]).reshape(1, QK_W)
    blk = np.arange(256) // ATT_HEAD_DIM
    bd = jnp.asarray((blk[:, None] == blk[None, :]).astype(np.float32), dtype=BF16)
    w2 = jnp.zeros((GATE_PAD_W, 2 * GLA_QK_W), F32)
    w2 = w2.at[:GLA_GATE_RANK, :GLA_QK_W].set(w_gate_fwd[0])
    w2 = w2.at[GLA_GATE_RANK:2 * GLA_GATE_RANK, GLA_QK_W:].set(w_gate_bwd[0]).astype(BF16)
    b2 = jnp.concatenate([b_gate_fwd[0], b_gate_bwd[0]]).reshape(1, 2 * GLA_QK_W)
    g1 = g_norm1[0].reshape(1, d)

    cos_t, sin_t = _rope_tables(seq)
    ones_t = jnp.ones((ctx_len, LANES), F32)
    zeros_t = jnp.zeros((ctx_len, LANES), F32)

    tm = 512
    tpb = seq // tm
    x2d = x.reshape(batch * seq, d)
    ctx2d = ctx.reshape(batch * ctx_len, d)

    shared = (g1, w_in_p, bd, gqk)
    _, ckt, cv, cgq, cgk, cgv, clf, clb, _ = _inproj(
        ctx2d, mod3, lambda b: batch, *shared, ones_t, zeros_t, w2, b2,
        tm=ctx_len, tiles_per_batch=1, batch=batch)
    q, kt, vext, gq, gk, gv, lf, lb, gr = _inproj(
        x2d, mod3, lambda b: b, *shared, cos_t, sin_t, w2, b2, tm=tm, tiles_per_batch=tpb, batch=batch)

    zero_state = jnp.zeros((batch, 2, GLA_HEADS * GLA_KEY_DIM, GLA_VAL_DIM), F32)
    _, _, s_ctx = _gla(cgq, cgk, cgv, clf, clb, zero_state, batch=batch)
    o_f, o_b, _ = _gla(gq, gk, gv, lf, lb, s_ctx, batch=batch)

    kt_all = jnp.concatenate([ckt, kt], axis=-1)
    v_all = jnp.concatenate([cv, vext], axis=2)
    score_bound = ATT_HEAD_DIM * jnp.max(jnp.abs(gqk[0, :ATT_Q_W])) * jnp.max(jnp.abs(gqk[0, ATT_Q_W:]))
    att_o = _attention(q, kt_all, v_all, score_bound, batch=batch, seq=seq)

    x1, h2 = _merge(att_o, o_f, o_b, gr, x2d, mod3, g_att_out[0].reshape(1, ATT_Q_W),
                    g_gla_out[0].reshape(1, GLA_VAL_DIM), w_out[0].astype(BF16), g_norm2[0].reshape(1, d),
                    tm=tm, tiles_per_batch=tpb)

    cw8 = jnp.zeros((8, conv_w.shape[-1]), F32).at[:CONV_WIDTH].set(conv_w[0])
    out = _ffn(h2, x1, mod3, g_final.reshape(1, d), w_up[0].astype(BF16), cw8,
               conv_b[0].reshape(1, -1), w_down[0].astype(BF16), tm=tm, tiles_per_batch=tpb)
    return out.reshape(batch, seq, d)
```
